```python
import math
import jax, jax.numpy as jnp
from jax import lax
import numpy as np

D_MODEL = 2048
BATCH = 2
SEQ = 4096
DEPTH = 1
DEC_BATCH = 128
DEC_SEQ = 8
PAST_LEN = 16384
PAGE_SIZE = 128

HEAD_DIM = 128
D_MIX = D_MODEL
MLA_HEADS = D_MIX // 2 // HEAD_DIM
DSA_HEADS = D_MIX // 4 // HEAD_DIM
MEM_HEADS = D_MIX // 4 // HEAD_DIM
MLA_WIDTH = MLA_HEADS * HEAD_DIM
DSA_WIDTH = DSA_HEADS * HEAD_DIM
MEM_WIDTH = MEM_HEADS * HEAD_DIM
Q_LORA = D_MODEL // 4
KV_LORA = D_MODEL // 8
QK_NOPE = HEAD_DIM
QK_ROPE = HEAD_DIM // 2
V_DIM = HEAD_DIM
ROPE_THETA = 10000.0
MLA_SCALE = (QK_NOPE + QK_ROPE) ** -0.5
DSA_KV_HEADS = 2
DSA_GROUP = DSA_HEADS // DSA_KV_HEADS
DSA_SCALE = HEAD_DIM ** -0.5
IDX_HEADS = 16
IDX_DIM = 64
IDX_SCALE = IDX_DIM ** -0.5
TOPK_MAX = 256
N_MEM = 256
MEM_SCALE = HEAD_DIM ** -0.5
NUM_BUCKETS = 32
MAX_DISTANCE = 128
Q_BLOCK = 128
EPS = 1e-6
SPLIT_SIZES = (Q_LORA, KV_LORA, QK_ROPE, MLA_WIDTH,
               DSA_WIDTH, DSA_KV_HEADS * HEAD_DIM, DSA_KV_HEADS * HEAD_DIM,
               IDX_HEADS * IDX_DIM, IDX_DIM, IDX_HEADS, DSA_WIDTH,
               MEM_WIDTH, MEM_WIDTH)
D_IN = sum(SPLIT_SIZES)

kernel_name = 'hymba_mla_dsa_mem_decode_step'


def rms_norm(x, g):
    xf = x.astype(jnp.float32)
    y = xf * lax.rsqrt(jnp.mean(xf * xf, axis=-1, keepdims=True) + EPS)
    return (y * g.astype(jnp.float32)).astype(x.dtype)


def rope(x, pos):
    half = QK_ROPE // 2
    inv_freq = jnp.exp(-math.log(ROPE_THETA) * jnp.arange(half, dtype=jnp.float32) / half)
    ang = pos.astype(jnp.float32)[:, None] * inv_freq[None, :]
    cos, sin = jnp.cos(ang)[:, None, :], jnp.sin(ang)[:, None, :]
    x1, x2 = x[..., :half].astype(jnp.float32), x[..., half:].astype(jnp.float32)
    return jnp.concatenate([x1 * cos - x2 * sin, x2 * cos + x1 * sin], axis=-1).astype(x.dtype)


def t5_bucket(dist):
    dist = jnp.maximum(dist, 0)
    max_exact = NUM_BUCKETS // 2
    log_ratio = jnp.log(jnp.maximum(dist, max_exact).astype(jnp.float32) / max_exact) / math.log(MAX_DISTANCE / max_exact)
    large = jnp.minimum(max_exact + (log_ratio * (NUM_BUCKETS - max_exact)).astype(jnp.int32), NUM_BUCKETS - 1)
    return jnp.where(dist < max_exact, dist, large)


def split_cols(z):
    outs, start = [], 0
    for size in SPLIT_SIZES:
        outs.append(z[..., start:start + size])
        start += size
    return outs


def branch_inputs(x, pos, norm_g, w_in, qa_norm_g, w_uq, kva_norm_g, w_ukv, idx_norm_g):
    b, t = x.shape[0], x.shape[1]
    z = jnp.einsum('btd,de->bte', rms_norm(x, norm_g), w_in)
    (c_q, c_kv, k_rope, g_mla, q_d, k_d, v_d, q_i, k_i, w_i, g_dsa, q_m, g_mem) = split_cols(z)
    q = jnp.einsum('btc,chd->bthd', rms_norm(c_q, qa_norm_g), w_uq)
    return dict(
        q_lat=jnp.einsum('bthd,chd->bthc', q[..., :QK_NOPE], w_ukv[..., :QK_NOPE]),
        q_rope=rope(q[..., QK_NOPE:], pos),
        ckv=rms_norm(c_kv, kva_norm_g),
        krope=rope(k_rope[:, :, None, :], pos)[:, :, 0, :],
        g_mla=g_mla,
        q_d=q_d.reshape(b, t, DSA_HEADS, HEAD_DIM),
        k_d=k_d.reshape(b, t, DSA_KV_HEADS, HEAD_DIM),
        v_d=v_d.reshape(b, t, DSA_KV_HEADS, HEAD_DIM),
        q_i=q_i.reshape(b, t, IDX_HEADS, IDX_DIM),
        k_i=rms_norm(k_i, idx_norm_g),
        w_i=w_i * IDX_HEADS ** -0.5,
        g_dsa=g_dsa,
        q_m=q_m.reshape(b, t, MEM_HEADS, HEAD_DIM),
        g_mem=g_mem)


def mla_attend(q_lat, q_rope, ckv, krope, q_pos, k_pos):
    logits = (jnp.einsum('...thc,...sc->...ths', q_lat, ckv)
              + jnp.einsum('...thr,...sr->...ths', q_rope, krope)).astype(jnp.float32) * MLA_SCALE
    mask = (k_pos[None, :] <= q_pos[:, None])[:, None, :]
    p = jax.nn.softmax(jnp.where(mask, logits, -jnp.inf), axis=-1).astype(ckv.dtype)
    return jnp.einsum('...ths,...sc->...thc', p, ckv)


def indexer_select(q_i, w_i, k_i, q_pos, k_pos, topk):
    dots = jnp.einsum('...thd,...sd->...ths', q_i, k_i).astype(jnp.float32) * IDX_SCALE
    scores = jnp.einsum('...ths,...th->...ts', jax.nn.relu(dots), w_i.astype(jnp.float32))
    scores = jnp.where(k_pos[None, :] <= q_pos[:, None], scores, -jnp.inf)
    _, idx = lax.top_k(scores, topk)
    return idx


def dsa_attend(q, k_sel, v_sel, sel_idx, q_pos, t5_bias):
    qg = q.reshape(q.shape[:-2] + (DSA_KV_HEADS, DSA_GROUP, HEAD_DIM))
    logits = jnp.einsum('...tgrd,...tkgd->...tgrk', qg, k_sel).astype(jnp.float32) * DSA_SCALE
    dist = q_pos[:, None] - sel_idx
    bias = jnp.moveaxis(t5_bias[t5_bucket(dist)], -1, -2)
    logits = logits + bias.reshape(logits.shape).astype(jnp.float32)
    logits = jnp.where((dist >= 0)[..., :, None, None, :], logits, -jnp.inf)
    p = jax.nn.softmax(logits, axis=-1).astype(v_sel.dtype)
    o = jnp.einsum('...tgrk,...tkgd->...tgrd', p, v_sel)
    return o.reshape(o.shape[:-3] + (DSA_WIDTH,))


def mem_kv(mem, mem_norm_g, w_mem_kv):
    kv = jnp.einsum('bnd,de->bne', rms_norm(mem, mem_norm_g), w_mem_kv)
    b, n = mem.shape[0], mem.shape[1]
    return (kv[..., :MEM_WIDTH].reshape(b, n, MEM_HEADS, HEAD_DIM),
            kv[..., MEM_WIDTH:].reshape(b, n, MEM_HEADS, HEAD_DIM))


def mem_attend(q, k, v):
    logits = jnp.einsum('bthd,bnhd->bthn', q, k).astype(jnp.float32) * MEM_SCALE
    p = jax.nn.softmax(logits, axis=-1).astype(v.dtype)
    o = jnp.einsum('bthn,bnhd->bthd', p, v)
    return o.reshape(o.shape[0], o.shape[1], MEM_WIDTH)


def merge_out(x, lat, o_d, o_m, br, w_ukv, w_out):
    b, t = x.shape[0], x.shape[1]
    o_mla = jnp.einsum('bthc,chd->bthd', lat, w_ukv[..., QK_NOPE:]).reshape(b, t, MLA_WIDTH)
    o = jnp.concatenate([o_mla * jax.nn.silu(br['g_mla']),
                         o_d * jax.nn.silu(br['g_dsa']),
                         o_m * jax.nn.silu(br['g_mem'])], axis=-1)
    return x + jnp.einsum('bte,ed->btd', o, w_out)


def setup_inputs(seed: int = 0) -> dict:
    key = jax.random.key(seed)
    ks = jax.random.split(key, 24)
    f32 = jnp.float32
    n_pages = PAST_LEN // PAGE_SIZE
    n_phys = (DEC_BATCH * n_pages * 5) // 4

    def nrm(k, shape, scale=1.0):
        return scale * jax.random.normal(k, shape, f32)

    def gain(k, shape):
        return 1.0 + 0.02 * jax.random.normal(k, shape, f32)

    page_table = jax.random.permutation(ks[10], n_phys)[: DEC_BATCH * n_pages].reshape(DEC_BATCH, n_pages).astype(jnp.int32)
    return {
        'x_prompt': nrm(ks[0], (BATCH, SEQ, D_MODEL)),
        'x_sample': nrm(ks[1], (DEC_BATCH, DEC_SEQ, D_MODEL)),
        'mem_prompt': nrm(ks[2], (BATCH, N_MEM, D_MODEL)),
        'cache_mla_ckv': nrm(ks[3], (DEPTH, n_phys, PAGE_SIZE, KV_LORA)),
        'cache_mla_krope': nrm(ks[4], (DEPTH, n_phys, PAGE_SIZE, QK_ROPE)),
        'cache_dsa_k': nrm(ks[5], (DEPTH, n_phys, PAGE_SIZE, DSA_KV_HEADS, HEAD_DIM)),
        'cache_dsa_v': nrm(ks[6], (DEPTH, n_phys, PAGE_SIZE, DSA_KV_HEADS, HEAD_DIM)),
        'cache_idx_k': nrm(ks[7], (DEPTH, n_phys, PAGE_SIZE, IDX_DIM)),
        'cache_mem_k': nrm(ks[8], (DEPTH, DEC_BATCH, N_MEM, MEM_HEADS, HEAD_DIM)),
        'cache_mem_v': nrm(ks[9], (DEPTH, DEC_BATCH, N_MEM, MEM_HEADS, HEAD_DIM)),
        'page_table': page_table,
        't5_bias': nrm(ks[11], (NUM_BUCKETS, DSA_HEADS), 0.2),
        'norm_g': gain(ks[12], (DEPTH, D_MODEL)),
        'w_in': nrm(ks[13], (DEPTH, D_MODEL, D_IN), D_MODEL ** -0.5),
        'qa_norm_g': gain(ks[14], (DEPTH, Q_LORA)),
        'w_uq': nrm(ks[15], (DEPTH, Q_LORA, MLA_HEADS, QK_NOPE + QK_ROPE), Q_LORA ** -0.5),
        'kva_norm_g': gain(ks[16], (DEPTH, KV_LORA)),
        'w_ukv': nrm(ks[17], (DEPTH, KV_LORA, MLA_HEADS, QK_NOPE + V_DIM), KV_LORA ** -0.5),
        'idx_norm_g': gain(ks[18], (DEPTH, IDX_DIM)),
        'mem_norm_g': gain(ks[19], (DEPTH, D_MODEL)),
        'w_mem_kv': nrm(ks[20], (DEPTH, D_MODEL, 2 * MEM_WIDTH), D_MODEL ** -0.5),
        'w_out': nrm(ks[21], (DEPTH, D_MIX, D_MODEL), D_MIX ** -0.5),
        'final_norm_g': gain(ks[22], (D_MODEL,)),
    }


def reference(x_prompt, x_sample, mem_prompt, cache_mla_ckv, cache_mla_krope, cache_dsa_k, cache_dsa_v,
              cache_idx_k, cache_mem_k, cache_mem_v, page_table, t5_bias, norm_g, w_in, qa_norm_g, w_uq,
              kva_norm_g, w_ukv, idx_norm_g, mem_norm_g, w_mem_kv, w_out, final_norm_g):
    bp, seq = x_prompt.shape[0], x_prompt.shape[1]
    dec_seq = x_sample.shape[1]
    past = page_table.shape[1] * PAGE_SIZE
    n_blocks = seq // Q_BLOCK
    pos_p = jnp.arange(seq, dtype=jnp.int32)
    pos_s = past + jnp.arange(dec_seq, dtype=jnp.int32)
    k_pos_s = jnp.arange(past + dec_seq, dtype=jnp.int32)
    topk_p = min(TOPK_MAX, seq // 4)
    topk_s = min(TOPK_MAX, (past + dec_seq) // 4)

    def to_blocks(a):
        return jnp.moveaxis(a.reshape((bp, n_blocks, Q_BLOCK) + a.shape[2:]), 1, 0)

    def from_blocks(a):
        return jnp.moveaxis(a, 0, 1).reshape((bp, seq) + a.shape[3:])

    gather_rows = jax.vmap(lambda a, i: a[i])

    xp, xs = x_prompt, x_sample
    st = {n: [] for n in ('ckv_p', 'krope_p', 'dk_p', 'dv_p', 'ik_p', 'mk_p', 'mv_p',
                          'ckv_s', 'krope_s', 'dk_s', 'dv_s', 'ik_s')}
    for l in range(DEPTH):
        P = branch_inputs(xp, pos_p, norm_g[l], w_in[l], qa_norm_g[l], w_uq[l], kva_norm_g[l], w_ukv[l], idx_norm_g[l])

        def prompt_block(args, P=P):
            q_lat, q_rope, q_d, q_i, w_i, q_pos = args
            lat = mla_attend(q_lat, q_rope, P['ckv'], P['krope'], q_pos, pos_p)
            idx = indexer_select(q_i, w_i, P['k_i'], q_pos, pos_p, topk_p)
            o_d = dsa_attend(q_d, gather_rows(P['k_d'], idx), gather_rows(P['v_d'], idx), idx, q_pos, t5_bias)
            return lat, o_d

        lat_b, o_d_b = lax.map(prompt_block, (to_blocks(P['q_lat']), to_blocks(P['q_rope']), to_blocks(P['q_d']),
                                              to_blocks(P['q_i']), to_blocks(P['w_i']), pos_p.reshape(n_blocks, Q_BLOCK)))
        mk_p, mv_p = mem_kv(mem_prompt, mem_norm_g[l], w_mem_kv[l])
        o_m_p = mem_attend(P['q_m'], mk_p, mv_p)
        xp = merge_out(xp, from_blocks(lat_b), from_blocks(o_d_b), o_m_p, P, w_ukv[l], w_out[l])

        S = branch_inputs(xs, pos_s, norm_g[l], w_in[l], qa_norm_g[l], w_uq[l], kva_norm_g[l], w_ukv[l], idx_norm_g[l])
        ckv_pool, krope_pool = cache_mla_ckv[l], cache_mla_krope[l]
        dk_pool, dv_pool, ik_pool = cache_dsa_k[l], cache_dsa_v[l], cache_idx_k[l]

        def sample_seq(args, ckv_pool=ckv_pool, krope_pool=krope_pool, dk_pool=dk_pool, dv_pool=dv_pool, ik_pool=ik_pool):
            pages, q_lat, q_rope, ckv_new, krope_new, q_d, k_new, v_new, q_i, w_i, ik_new = args
            ckv = jnp.concatenate([ckv_pool[pages].reshape(past, KV_LORA), ckv_new], axis=0)
            krope = jnp.concatenate([krope_pool[pages].reshape(past, QK_ROPE), krope_new], axis=0)
            lat = mla_attend(q_lat, q_rope, ckv, krope, pos_s, k_pos_s)
            k_i = jnp.concatenate([ik_pool[pages].reshape(past, IDX_DIM), ik_new], axis=0)
            idx = indexer_select(q_i, w_i, k_i, pos_s, k_pos_s, topk_s)
            in_past = (idx < past)[..., None, None]
            page = pages[jnp.minimum(idx, past - 1) // PAGE_SIZE]
            off = idx % PAGE_SIZE
            row = jnp.clip(idx - past, 0, dec_seq - 1)
            k_sel = jnp.where(in_past, dk_pool[page, off], k_new[row])
            v_sel = jnp.where(in_past, dv_pool[page, off], v_new[row])
            o_d = dsa_attend(q_d, k_sel, v_sel, idx, pos_s, t5_bias)
            return lat, o_d

        lat_s, o_d_s = lax.map(sample_seq, (page_table, S['q_lat'], S['q_rope'], S['ckv'], S['krope'], S['q_d'],
                                            S['k_d'], S['v_d'], S['q_i'], S['w_i'], S['k_i']))
        o_m_s = mem_attend(S['q_m'], cache_mem_k[l], cache_mem_v[l])
        xs = merge_out(xs, lat_s, o_d_s, o_m_s, S, w_ukv[l], w_out[l])

        st['ckv_p'].append(P['ckv']); st['krope_p'].append(P['krope'])
        st['dk_p'].append(P['k_d']); st['dv_p'].append(P['v_d']); st['ik_p'].append(P['k_i'])
        st['mk_p'].append(mk_p); st['mv_p'].append(mv_p)
        st['ckv_s'].append(S['ckv']); st['krope_s'].append(S['krope'])
        st['dk_s'].append(S['k_d']); st['dv_s'].append(S['v_d']); st['ik_s'].append(S['k_i'])

    y_prompt = rms_norm(xp, final_norm_g)
    y_sample = rms_norm(xs, final_norm_g)
    ckv_p, krope_p = jnp.stack(st['ckv_p']), jnp.stack(st['krope_p'])
    dsa_k_p, dsa_v_p, idx_k_p = jnp.stack(st['dk_p']), jnp.stack(st['dv_p']), jnp.stack(st['ik_p'])
    mem_k_p, mem_v_p = jnp.stack(st['mk_p']), jnp.stack(st['mv_p'])
    ckv_s, krope_s = jnp.stack(st['ckv_s']), jnp.stack(st['krope_s'])
    dsa_k_s, dsa_v_s, idx_k_s = jnp.stack(st['dk_s']), jnp.stack(st['dv_s']), jnp.stack(st['ik_s'])
    return (y_prompt, y_sample, ckv_p, krope_p, dsa_k_p, dsa_v_p, idx_k_p, mem_k_p, mem_v_p,
            ckv_s, krope_s, dsa_k_s, dsa_v_s, idx_k_s)
```

```python
import functools
import math

import numpy as np
import jax
import jax.numpy as jnp
from jax import lax
from jax.experimental import pallas as pl
from jax.experimental.pallas import tpu as pltpu

F32 = jnp.float32
BF16 = jnp.bfloat16
I32 = jnp.int32

HEAD_DIM = 128
MLA_HEADS = 8
DSA_HEADS = 4
DSA_KV_HEADS = 2
MEM_HEADS = 4
Q_LORA = 512
KV_LORA = 256
QK_ROPE = 64
IDX_HEADS = 16
IDX_DIM = 64
TOPK_MAX = 256
PAGE_SIZE = 128
NUM_BUCKETS = 32
ROPE_THETA = 10000.0
MLA_SCALE = (HEAD_DIM + QK_ROPE) ** -0.5
DSA_SCALE = HEAD_DIM ** -0.5
MEM_SCALE = HEAD_DIM ** -0.5
IDX_W_SCALE = (IDX_DIM ** -0.5) * (IDX_HEADS ** -0.5)
EPS = 1e-6
NEG_INF = float("-inf")
F32_MAX = float(np.finfo(np.float32).max)

C_Q, C_KV, C_KRKI, C_WI, C_GMLA, C_QD, C_KD, C_VD, C_QI, C_GDSA, C_QM, C_GMEM, D_Z = (
    0, 512, 768, 896, 1024, 2048, 2560, 2816, 3072, 4096, 4608, 5120, 5632)

T5_BUCKET_START = tuple(range(17)) + (19, 21, 24, 27, 31, 35, 40, 46, 52, 59, 67, 77, 87, 99, 113)
T5_FAR = T5_BUCKET_START[-1]

VMEM_LIMIT = 56 * 1024 * 1024


def _cparams(sem):
    return pltpu.CompilerParams(dimension_semantics=sem, vmem_limit_bytes=VMEM_LIMIT)


def _dot_nt(a, b):
    return lax.dot_general(a, b, (((1,), (1,)), ((), ())), preferred_element_type=F32)


def _dot(a, b):
    return jnp.dot(a, b, preferred_element_type=F32)


def _rms(x, g):
    return x * lax.rsqrt(jnp.mean(x * x, axis=-1, keepdims=True) + EPS) * g


def _inproj_kernel(x_ref, g_ref, w_ref, z_ref, h_ref):
    @pl.when(pl.program_id(1) == 0)
    def _():
        h_ref[...] = _rms(x_ref[...], g_ref[...]).astype(BF16)

    z_ref[...] = _dot(h_ref[...], w_ref[...])


def _in_proj(x, g, w, tn=512):
    n, d = x.shape
    m = w.shape[1]
    tm = min(512, n)
    return pl.pallas_call(
        _inproj_kernel,
        out_shape=jax.ShapeDtypeStruct((n, m), F32),
        grid=(n // tm, m // tn),
        in_specs=[pl.BlockSpec((tm, d), lambda i, j: (i, 0)),
                  pl.BlockSpec((1, d), lambda i, j: (0, 0)),
                  pl.BlockSpec((d, tn), lambda i, j: (0, j))],
        out_specs=pl.BlockSpec((tm, tn), lambda i, j: (i, j)),
        scratch_shapes=[pltpu.VMEM((tm, d), BF16)],
        compiler_params=_cparams(("parallel", "arbitrary")),
        name="in_proj",
    )(x, g, w)


def _post_kernel(cq_ref, ckv_ref, krki_ref, kd_ref, vd_ref, cos_ref, sin_ref, qag_ref, kvag_ref, idxg_ref,
                 wuq_ref, wk_ref,
                 ckv_o, krope_o, ki_o, kd_o, vd_o, kcat_o, kip0_o, kip1_o, kdb_o, vdb_o, qcat_o):
    tm = cq_ref.shape[0]
    lane = lax.broadcasted_iota(I32, (tm, 128), 1)
    hi = lane >= 64
    second_half = (lane % 64) >= 32
    cos_t = cos_ref[...]
    sin_t = sin_ref[...]

    def rope128(x):
        swapped = jnp.where(second_half, pltpu.roll(x, 32, 1), pltpu.roll(x, 96, 1))
        return x * cos_t + swapped * sin_t

    ckv_n = _rms(ckv_ref[...], kvag_ref[...])
    ckv_o[...] = ckv_n

    krki = krki_ref[...]
    rot = rope128(krki)
    krope_o[...] = rot[:, :QK_ROPE]
    ms = jnp.sum(jnp.where(hi, krki * krki, 0.0), axis=-1, keepdims=True) * (1.0 / IDX_DIM)
    ki_hi = krki * lax.rsqrt(ms + EPS) * idxg_ref[...]
    ki_lo = pltpu.roll(ki_hi, 64, 1)
    ki_o[...] = ki_lo[:, :IDX_DIM]
    kip0_o[...] = jnp.where(hi, 0.0, ki_lo).astype(BF16)
    kip1_o[...] = jnp.where(hi, ki_hi, 0.0).astype(BF16)
    kcat_o[:, :KV_LORA] = ckv_n.astype(BF16)
    kcat_o[:, KV_LORA:] = jnp.where(hi, 0.0, rot).astype(BF16)

    kd = kd_ref[...]
    vd = vd_ref[...]
    kd_o[...] = kd
    vd_o[...] = vd
    kdb_o[...] = kd.astype(BF16)
    vdb_o[...] = vd.astype(BF16)

    cqn = _rms(cq_ref[...], qag_ref[...]).astype(BF16)
    qall = _dot(cqn, wuq_ref[...])
    nope_w = MLA_HEADS * HEAD_DIM
    for h in range(MLA_HEADS):
        qn = qall[:, h * HEAD_DIM:(h + 1) * HEAD_DIM].astype(BF16)
        qcat_o[h, :, :KV_LORA] = _dot(qn, wk_ref[h]).astype(qcat_o.dtype)
    for c in range(MLA_HEADS // 2):
        rc = rope128(qall[:, nope_w + 128 * c:nope_w + 128 * (c + 1)])
        qcat_o[2 * c, :, KV_LORA:] = jnp.where(hi, 0.0, rc).astype(qcat_o.dtype)
        qcat_o[2 * c + 1, :, KV_LORA:] = jnp.where(hi, 0.0, pltpu.roll(rc, 64, 1)).astype(qcat_o.dtype)


def _post(z, cos_t, sin_t, qag, kvag, idxg128, wuq, wk, q_dtype):
    n = z.shape[0]
    tm = min(256, n)
    nt = cos_t.shape[0] // tm
    row = lambda c, w: pl.BlockSpec((tm, w), lambda i, c=c, w=w: (i, c // w))
    full2 = lambda a: pl.BlockSpec(a.shape, lambda i: (0, 0))
    full3 = lambda a: pl.BlockSpec(a.shape, lambda i: (0, 0, 0))
    tab = pl.BlockSpec((tm, 128), lambda i: (i % nt, 0))
    out_shapes = [
        jax.ShapeDtypeStruct((n, KV_LORA), F32), jax.ShapeDtypeStruct((n, QK_ROPE), F32),
        jax.ShapeDtypeStruct((n, IDX_DIM), F32), jax.ShapeDtypeStruct((n, 256), F32),
        jax.ShapeDtypeStruct((n, 256), F32), jax.ShapeDtypeStruct((n, 384), BF16),
        jax.ShapeDtypeStruct((n, 128), BF16), jax.ShapeDtypeStruct((n, 128), BF16),
        jax.ShapeDtypeStruct((n, 256), BF16), jax.ShapeDtypeStruct((n, 256), BF16),
        jax.ShapeDtypeStruct((MLA_HEADS, n, 384), q_dtype)]
    o2 = lambda w: pl.BlockSpec((tm, w), lambda i: (i, 0))
    out_specs = [o2(KV_LORA), o2(QK_ROPE), o2(IDX_DIM), o2(256), o2(256), o2(384), o2(128), o2(128), o2(256),
                 o2(256), pl.BlockSpec((MLA_HEADS, tm, 384), lambda i: (0, i, 0))]
    return pl.pallas_call(
        _post_kernel,
        out_shape=out_shapes,
        grid=(n // tm,),
        in_specs=[row(C_Q, 512), row(C_KV, 256), row(C_KRKI, 128), row(C_KD, 256), row(C_VD, 256), tab, tab,
                  full2(qag), full2(kvag), full2(idxg128), full2(wuq), full3(wk)],
        out_specs=out_specs,
        compiler_params=_cparams(("parallel",)),
        name="post",
    )(z, z, z, z, z, cos_t, sin_t, qag, kvag, idxg128, wuq, wk)


def _kth_largest(count_ge, k, rows):
    sign = np.int32(-2 ** 31)
    key_neg_inf = np.int32(-2139095041)

    def to_float(u):
        key = u ^ sign
        bits = jnp.where(key >= 0, key, key ^ np.int32(0x7FFFFFFF))
        return key, lax.bitcast_convert_type(bits, F32)

    def body(b, prefix):
        cand = prefix | jnp.left_shift(np.int32(1), (31 - b).astype(I32))
        key, cand_f = to_float(cand)
        ok = (count_ge(cand_f) >= k) | (key < key_neg_inf)
        return jnp.where(ok, cand, prefix)

    prefix = lax.fori_loop(0, 32, body, jnp.zeros((rows, 1), I32))
    return to_float(prefix)[1]


def _t5_bias_tile(dist, t5_ref, heads):
    ge = [dist >= T5_BUCKET_START[b] for b in range(1, NUM_BUCKETS)]
    out = []
    for h in heads:
        bias = jnp.full(dist.shape, t5_ref[0, h], F32)
        for b in range(1, NUM_BUCKETS):
            bias = jnp.where(ge[b - 1], t5_ref[b, h], bias)
        out.append(bias)
    return out


def _softmax_update(s, m, l):
    m_new = jnp.maximum(m, jnp.max(s, axis=-1, keepdims=True))
    m_use = jnp.where(m_new == NEG_INF, 0.0, m_new)
    alpha = jnp.exp(m - m_use)
    p = jnp.exp(s - m_use)
    return m_new, alpha * l + jnp.sum(p, axis=-1, keepdims=True), alpha, p


def _prompt_attn_kernel(t5_ref, qcat_ref, qd_ref, qi_ref, wi_ref, kcat_ref, kip0_ref, kip1_ref, kd_ref, vd_ref,
                        wv_ref, omla_ref, od_ref, sc_ref, acc_ref, accd_ref, *, tq, tk, topk):
    i = pl.program_id(1)
    q0 = i * tq
    nfull = q0 // tk
    rows_m = MLA_HEADS * tq

    row_t = q0 + lax.broadcasted_iota(I32, (tq, tk), 0)
    col_l = lax.broadcasted_iota(I32, (tq, tk), 1)

    qall = qcat_ref[...].reshape(rows_m, 384)
    qi_b = qi_ref[...].astype(BF16)
    wsc = wi_ref[...] * IDX_W_SCALE
    acc_ref[...] = jnp.zeros_like(acc_ref)

    def kv_step(j, carry, masked):
        m, l = carry
        start = pl.multiple_of(j * tk, tk)
        kc = kcat_ref[pl.ds(start, tk), :]
        if masked:
            vis = (start + col_l) <= row_t
        s = _dot_nt(qall, kc) * MLA_SCALE
        if masked:
            s = jnp.where(vis[None], s.reshape(MLA_HEADS, tq, tk), NEG_INF).reshape(rows_m, tk)
        m_new, l_new, alpha, p = _softmax_update(s, m, l)
        acc_ref[...] = acc_ref[...] * alpha + _dot(p.astype(BF16), kc[:, :KV_LORA])
        kp = (kip0_ref[pl.ds(start, tk), :], kip1_ref[pl.ds(start, tk), :])
        sc = jnp.zeros((tq, tk), F32)
        for h in range(IDX_HEADS):
            d = _dot_nt(qi_b[:, 128 * (h // 2):128 * (h // 2 + 1)], kp[h % 2])
            sc = sc + jnp.maximum(d, 0.0) * wsc[:, h:h + 1]
        if masked:
            sc = jnp.where(vis, sc, NEG_INF)
        sc_ref[j] = sc
        return m_new, l_new

    init = (jnp.full((rows_m, 1), NEG_INF, F32), jnp.zeros((rows_m, 1), F32))
    carry = lax.fori_loop(0, nfull, functools.partial(kv_step, masked=False), init)
    m_mla, l_mla = kv_step(nfull, carry, masked=True)

    def count_ge(cand):
        def body(j, c):
            x = sc_ref[j]
            for u in range(tk // 128):
                c = c + jnp.where(x[:, u * 128:(u + 1) * 128] >= cand, 1, 0)
            return c
        c = lax.fori_loop(0, nfull + 1, body, jnp.zeros((tq, 128), I32))
        return jnp.sum(c, axis=-1, keepdims=True)

    thr = jnp.maximum(_kth_largest(count_ge, topk, tq), -F32_MAX)

    qd_b = qd_ref[...].astype(BF16)
    qg = [jnp.concatenate([qd_b[:, (2 * g) * 128:(2 * g + 1) * 128],
                           qd_b[:, (2 * g + 1) * 128:(2 * g + 2) * 128]], axis=0) for g in range(DSA_KV_HEADS)]
    accd_ref[...] = jnp.zeros_like(accd_ref)

    def dsa_step(j, carry, near):
        start = pl.multiple_of(j * tk, tk)
        sel = sc_ref[j] >= thr
        if near:
            bias = _t5_bias_tile(row_t - (start + col_l), t5_ref, range(DSA_HEADS))
        new = []
        for g in range(DSA_KV_HEADS):
            m, l = carry[g]
            kd = kd_ref[pl.ds(start, tk), g * 128:(g + 1) * 128]
            vd = vd_ref[pl.ds(start, tk), g * 128:(g + 1) * 128]
            s = _dot_nt(qg[g], kd) * DSA_SCALE
            parts = []
            for r in range(2):
                h = 2 * g + r
                b = bias[h] if near else t5_ref[NUM_BUCKETS - 1, h]
                parts.append(jnp.where(sel, s[r * tq:(r + 1) * tq] + b, NEG_INF))
            s = jnp.concatenate(parts, axis=0)
            m_new, l_new, alpha, p = _softmax_update(s, m, l)
            accd_ref[g] = accd_ref[g] * alpha + _dot(p.astype(BF16), vd)
            new.append((m_new, l_new))
        return tuple(new)

    nfar = jnp.maximum((q0 - T5_FAR + 1) // tk, 0)
    init_d = tuple((jnp.full((2 * tq, 1), NEG_INF, F32), jnp.zeros((2 * tq, 1), F32)) for _ in range(DSA_KV_HEADS))
    carry_d = lax.fori_loop(0, nfar, functools.partial(dsa_step, near=False), init_d)
    carry_d = lax.fori_loop(nfar, nfull + 1, functools.partial(dsa_step, near=True), carry_d)

    for g in range(DSA_KV_HEADS):
        o = accd_ref[g] / carry_d[g][1]
        for r in range(2):
            h = 2 * g + r
            od_ref[:, h * 128:(h + 1) * 128] = o[r * tq:(r + 1) * tq]
    lat = (acc_ref[...] / l_mla).astype(BF16)
    for h in range(MLA_HEADS):
        omla_ref[:, h * 128:(h + 1) * 128] = _dot(lat[h * tq:(h + 1) * tq], wv_ref[h])


def _prompt_attn(t5, qcat8, z, kcat, kip0, kip1, kdb, vdb, wv, *, batch, seq, tq, tk, topk):
    n = batch * seq
    nq = seq // tq
    qrow = lambda c, w: pl.BlockSpec((tq, w), lambda b, i, c=c, w=w: (b * nq + i, c // w))
    kfull = lambda w: pl.BlockSpec((seq, w), lambda b, i: (b, 0))
    kern = functools.partial(_prompt_attn_kernel, tq=tq, tk=tk, topk=topk)
    return pl.pallas_call(
        kern,
        out_shape=[jax.ShapeDtypeStruct((n, MLA_HEADS * HEAD_DIM), F32),
                   jax.ShapeDtypeStruct((n, DSA_HEADS * HEAD_DIM), F32)],
        grid=(batch, nq),
        in_specs=[pl.BlockSpec(memory_space=pltpu.SMEM),
                  pl.BlockSpec((MLA_HEADS, tq, 384), lambda b, i: (0, b * nq + i, 0)),
                  qrow(C_QD, 512), qrow(C_QI, 1024), qrow(C_WI, 128),
                  kfull(384), kfull(128), kfull(128), kfull(256), kfull(256),
                  pl.BlockSpec(wv.shape, lambda b, i: (0, 0, 0))],
        out_specs=[pl.BlockSpec((tq, MLA_HEADS * HEAD_DIM), lambda b, i: (b * nq + i, 0)),
                   pl.BlockSpec((tq, DSA_HEADS * HEAD_DIM), lambda b, i: (b * nq + i, 0))],
        scratch_shapes=[pltpu.VMEM((seq // tk, tq, tk), F32),
                        pltpu.VMEM((MLA_HEADS * tq, KV_LORA), F32),
                        pltpu.VMEM((DSA_KV_HEADS, 2 * tq, HEAD_DIM), F32)],
        compiler_params=_cparams(("parallel", "arbitrary")),
        name="prompt_attn",
    )(t5, qcat8, z, z, z, kcat, kip0, kip1, kdb, vdb, wv)


def _pad_rows(x, rows):
    return jnp.concatenate([x, jnp.zeros((rows - x.shape[0], x.shape[1]), x.dtype)], axis=0)


def _sample_mla_kernel(pt_ref, qcat_ref, qi_ref, we_ref, wo_ref, ckvn_ref, krn_ref, kin_ref, wv_ref, *rest, g_pages):
    ckv_refs = rest[:g_pages]
    kr_refs = rest[g_pages:2 * g_pages]
    ki_refs = rest[2 * g_pages:3 * g_pages]
    sc_ref, omla_ref, m_ref, l_ref, acc_ref = rest[3 * g_pages:]
    j = pl.program_id(1)
    ng = pl.num_programs(1) - 1
    t_new = ckvn_ref.shape[0]
    rows_m = MLA_HEADS * t_new

    @pl.when(j == 0)
    def _():
        m_ref[...] = jnp.full_like(m_ref, NEG_INF)
        l_ref[...] = jnp.zeros_like(l_ref)
        acc_ref[...] = jnp.zeros_like(acc_ref)

    q = qcat_ref[...].reshape(rows_m, 384).astype(BF16)
    q_even = qi_ref[0].astype(BF16)
    q_odd = qi_ref[1].astype(BF16)
    w_even = we_ref[...] * IDX_W_SCALE
    w_odd = wo_ref[...] * IDX_W_SCALE

    def process(pages, mask):
        s_parts, sc_parts, cks = [], [], []
        for ck, kr, ki in pages:
            ck_b = ck.astype(BF16)
            cks.append(ck_b)
            s = (_dot_nt(q[:, :KV_LORA], ck_b) + _dot_nt(q[:, KV_LORA:KV_LORA + QK_ROPE], kr.astype(BF16))) * MLA_SCALE
            ki_b = ki.astype(BF16)
            d_even = jnp.maximum(_dot_nt(q_even, ki_b), 0.0) * w_even
            d_odd = jnp.maximum(_dot_nt(q_odd, ki_b), 0.0) * w_odd
            sc = jnp.sum((d_even + d_odd).reshape(IDX_HEADS // 2, t_new, PAGE_SIZE), axis=0)
            if mask is not None:
                s = jnp.where(mask[None], s.reshape(MLA_HEADS, t_new, PAGE_SIZE), NEG_INF).reshape(rows_m, PAGE_SIZE)
                sc = jnp.where(mask, sc, NEG_INF)
            s_parts.append(s)
            sc_parts.append(sc)
        s = jnp.concatenate(s_parts, axis=1) if len(s_parts) > 1 else s_parts[0]
        m_new, l_new, alpha, p = _softmax_update(s, m_ref[...], l_ref[...])
        p = p.astype(BF16)
        pv = _dot(p[:, :PAGE_SIZE], cks[0])
        for u in range(1, len(cks)):
            pv = pv + _dot(p[:, u * PAGE_SIZE:(u + 1) * PAGE_SIZE], cks[u])
        acc_ref[...] = acc_ref[...] * alpha + pv
        m_ref[...] = m_new
        l_ref[...] = l_new
        return sc_parts

    @pl.when(j < ng)
    def _():
        pages = [(ckv_refs[u][...], kr_refs[u][...], ki_refs[u][...]) for u in range(g_pages)]
        sc_parts = process(pages, None)
        for u in range(g_pages):
            sc_ref[:, u * PAGE_SIZE:(u + 1) * PAGE_SIZE] = sc_parts[u]

    @pl.when(j == ng)
    def _():
        tok = lax.broadcasted_iota(I32, (t_new, PAGE_SIZE), 0)
        col = lax.broadcasted_iota(I32, (t_new, PAGE_SIZE), 1)
        page = (_pad_rows(ckvn_ref[...], PAGE_SIZE), _pad_rows(krn_ref[...], PAGE_SIZE),
                _pad_rows(kin_ref[...], PAGE_SIZE))
        sc_new = process([page], col <= tok)[0]
        sc_ref[...] = jnp.full(sc_ref.shape, NEG_INF, F32)
        sc_ref[:, :PAGE_SIZE] = sc_new
        lat = (acc_ref[...] / l_ref[...]).astype(BF16)
        for h in range(MLA_HEADS):
            omla_ref[:, h * 128:(h + 1) * 128] = _dot(lat[h * t_new:(h + 1) * t_new], wv_ref[h])


def _sample_mla(page_table, qcat8, qi_eo, w_even, w_odd, ckv_new, kr_new, ki_new, wv, ckv_pool, kr_pool, ki_pool,
                *, nb, t_new, g_pages):
    npages = page_table.shape[1]
    ng = npages // g_pages
    gw = g_pages * PAGE_SIZE
    kern = functools.partial(_sample_mla_kernel, g_pages=g_pages)
    rows_i = (IDX_HEADS // 2) * t_new

    def page_spec(w, u):
        return pl.BlockSpec((None, None, PAGE_SIZE, w),
                            lambda n, j, pt, u=u: (0, pt[n, jnp.minimum(j, ng - 1) * g_pages + u], 0, 0))

    in_specs = [pl.BlockSpec((MLA_HEADS, t_new, 384), lambda n, j, pt: (0, n, 0)),
                pl.BlockSpec((None, 2, rows_i, IDX_DIM), lambda n, j, pt: (n, 0, 0, 0)),
                pl.BlockSpec((None, rows_i, 1), lambda n, j, pt: (n, 0, 0)),
                pl.BlockSpec((None, rows_i, 1), lambda n, j, pt: (n, 0, 0)),
                pl.BlockSpec((t_new, KV_LORA), lambda n, j, pt: (n, 0)),
                pl.BlockSpec((t_new, QK_ROPE), lambda n, j, pt: (n, 0)),
                pl.BlockSpec((t_new, IDX_DIM), lambda n, j, pt: (n, 0)),
                pl.BlockSpec(wv.shape, lambda n, j, pt: (0, 0, 0))]
    in_specs += [page_spec(KV_LORA, u) for u in range(g_pages)]
    in_specs += [page_spec(QK_ROPE, u) for u in range(g_pages)]
    in_specs += [page_spec(IDX_DIM, u) for u in range(g_pages)]
    grid_spec = pltpu.PrefetchScalarGridSpec(
        num_scalar_prefetch=1, grid=(nb, ng + 1), in_specs=in_specs,
        out_specs=[pl.BlockSpec((None, t_new, gw), lambda n, j, pt: (n, 0, j)),
                   pl.BlockSpec((t_new, MLA_HEADS * HEAD_DIM), lambda n, j, pt: (n, 0))],
        scratch_shapes=[pltpu.VMEM((MLA_HEADS * t_new, 1), F32), pltpu.VMEM((MLA_HEADS * t_new, 1), F32),
                        pltpu.VMEM((MLA_HEADS * t_new, KV_LORA), F32)])
    return pl.pallas_call(
        kern,
        out_shape=[jax.ShapeDtypeStruct((nb, t_new, (ng + 1) * gw), F32),
                   jax.ShapeDtypeStruct((nb * t_new, MLA_HEADS * HEAD_DIM), F32)],
        grid_spec=grid_spec,
        compiler_params=_cparams(("parallel", "arbitrary")),
        name="sample_mla",
    )(page_table, qcat8, qi_eo, w_even, w_odd, ckv_new, kr_new, ki_new, wv,
      *([ckv_pool] * g_pages), *([kr_pool] * g_pages), *([ki_pool] * g_pages))


def _sample_thr_kernel(sc_ref, thr_ref, *, topk, chunk):
    nseq, t_new, width = sc_ref.shape
    rows = nseq * t_new

    def count_ge(cand):
        def body(c_idx, c):
            x = sc_ref[:, :, pl.ds(pl.multiple_of(c_idx * chunk, chunk), chunk)].reshape(rows, chunk)
            for u in range(chunk // 128):
                c = c + jnp.where(x[:, u * 128:(u + 1) * 128] >= cand, 1, 0)
            return c
        c = lax.fori_loop(0, width // chunk, body, jnp.zeros((rows, 128), I32))
        return jnp.sum(c, axis=-1, keepdims=True)

    thr_ref[...] = jnp.maximum(_kth_largest(count_ge, topk, rows), -F32_MAX)


def _sample_thr(scores, *, topk, nseq=8, chunk=512):
    nb, t_new, width = scores.shape
    nseq = min(nseq, nb)
    chunk = min(chunk, width)
    return pl.pallas_call(
        functools.partial(_sample_thr_kernel, topk=topk, chunk=chunk),
        out_shape=jax.ShapeDtypeStruct((nb * t_new, 1), F32),
        grid=(nb // nseq,),
        in_specs=[pl.BlockSpec((nseq, t_new, width), lambda i: (i, 0, 0))],
        out_specs=pl.BlockSpec((nseq * t_new, 1), lambda i: (i, 0)),
        compiler_params=_cparams(("parallel",)),
        name="sample_thr",
    )(scores)


def _sample_dsa_kernel(pt_ref, t5_ref, qd_ref, sc_ref, thr_ref, kdn_ref, vdn_ref, *rest, g_pages, past):
    kd_refs = rest[:g_pages]
    vd_refs = rest[g_pages:2 * g_pages]
    od_ref, m_ref, l_ref, acc_ref = rest[2 * g_pages:]
    j = pl.program_id(1)
    ng = pl.num_programs(1) - 1
    t_new = qd_ref.shape[0]
    gw = g_pages * PAGE_SIZE

    @pl.when(j == 0)
    def _():
        m_ref[...] = jnp.full_like(m_ref, NEG_INF)
        l_ref[...] = jnp.zeros_like(l_ref)
        acc_ref[...] = jnp.zeros_like(acc_ref)

    qd_b = qd_ref[...].astype(BF16)
    qg = [jnp.concatenate([qd_b[:, (2 * g) * 128:(2 * g + 1) * 128],
                           qd_b[:, (2 * g + 1) * 128:(2 * g + 2) * 128]], axis=0) for g in range(DSA_KV_HEADS)]
    thr = thr_ref[...]

    def process(k_tiles, v_tiles, sel, dist):
        if dist is not None:
            bias = _t5_bias_tile(dist, t5_ref, range(DSA_HEADS))
        for g in range(DSA_KV_HEADS):
            s = jnp.concatenate([_dot_nt(qg[g], kt) for kt in k_tiles[g]], axis=1) * DSA_SCALE
            parts = []
            for r in range(2):
                h = 2 * g + r
                b = bias[h] if dist is not None else t5_ref[NUM_BUCKETS - 1, h]
                parts.append(jnp.where(sel, s[r * t_new:(r + 1) * t_new] + b, NEG_INF))
            s = jnp.concatenate(parts, axis=0)
            m_new, l_new, alpha, p = _softmax_update(s, m_ref[g], l_ref[g])
            p = p.astype(BF16)
            pv = _dot(p[:, :PAGE_SIZE], v_tiles[g][0])
            for u in range(1, len(v_tiles[g])):
                pv = pv + _dot(p[:, u * PAGE_SIZE:(u + 1) * PAGE_SIZE], v_tiles[g][u])
            acc_ref[g] = acc_ref[g] * alpha + pv
            m_ref[g] = m_new
            l_ref[g] = l_new

    def past_step(near):
        k_tiles = [[kd_refs[u][:, g, :].astype(BF16) for u in range(g_pages)] for g in range(DSA_KV_HEADS)]
        v_tiles = [[vd_refs[u][:, g, :].astype(BF16) for u in range(g_pages)] for g in range(DSA_KV_HEADS)]
        sel = sc_ref[...] >= thr
        dist = None
        if near:
            tok = lax.broadcasted_iota(I32, (t_new, gw), 0)
            col = lax.broadcasted_iota(I32, (t_new, gw), 1)
            dist = (past + tok) - (j * gw + col)
        process(k_tiles, v_tiles, sel, dist)

    @pl.when(j < ng - 1)
    def _():
        past_step(False)

    @pl.when(j == ng - 1)
    def _():
        past_step(True)

    @pl.when(j == ng)
    def _():
        tok = lax.broadcasted_iota(I32, (t_new, PAGE_SIZE), 0)
        col = lax.broadcasted_iota(I32, (t_new, PAGE_SIZE), 1)
        kdn = _pad_rows(kdn_ref[...], PAGE_SIZE).astype(BF16)
        vdn = _pad_rows(vdn_ref[...], PAGE_SIZE).astype(BF16)
        k_tiles = [[kdn[:, g * 128:(g + 1) * 128]] for g in range(DSA_KV_HEADS)]
        v_tiles = [[vdn[:, g * 128:(g + 1) * 128]] for g in range(DSA_KV_HEADS)]
        sel = sc_ref[:, :PAGE_SIZE] >= thr
        process(k_tiles, v_tiles, sel, tok - col)
        for g in range(DSA_KV_HEADS):
            o = acc_ref[g] / l_ref[g]
            for r in range(2):
                h = 2 * g + r
                od_ref[:, h * 128:(h + 1) * 128] = o[r * t_new:(r + 1) * t_new]


def _sample_dsa(page_table, t5, z, scores, thr, kd_new, vd_new, kd_pool, vd_pool, *, nb, t_new, g_pages):
    npages = page_table.shape[1]
    ng = npages // g_pages
    gw = g_pages * PAGE_SIZE
    kern = functools.partial(_sample_dsa_kernel, g_pages=g_pages, past=npages * PAGE_SIZE)

    def page_spec(u):
        return pl.BlockSpec((None, None, PAGE_SIZE, DSA_KV_HEADS, HEAD_DIM),
                            lambda n, j, pt: (0, pt[n, jnp.minimum(j, ng - 1) * g_pages + u], 0, 0, 0))

    in_specs = [pl.BlockSpec(memory_space=pltpu.SMEM),
                pl.BlockSpec((t_new, 512), lambda n, j, pt: (n, C_QD // 512)),
                pl.BlockSpec((None, t_new, gw), lambda n, j, pt: (n, 0, j)),
                pl.BlockSpec((t_new, 1), lambda n, j, pt: (n, 0)),
                pl.BlockSpec((t_new, 256), lambda n, j, pt: (n, 0)),
                pl.BlockSpec((t_new, 256), lambda n, j, pt: (n, 0))]
    in_specs += [page_spec(u) for u in range(g_pages)] + [page_spec(u) for u in range(g_pages)]
    grid_spec = pltpu.PrefetchScalarGridSpec(
        num_scalar_prefetch=1, grid=(nb, ng + 1), in_specs=in_specs,
        out_specs=pl.BlockSpec((t_new, DSA_HEADS * HEAD_DIM), lambda n, j, pt: (n, 0)),
        scratch_shapes=[pltpu.VMEM((DSA_KV_HEADS, 2 * t_new, 1), F32), pltpu.VMEM((DSA_KV_HEADS, 2 * t_new, 1), F32),
                        pltpu.VMEM((DSA_KV_HEADS, 2 * t_new, HEAD_DIM), F32)])
    return pl.pallas_call(
        kern,
        out_shape=jax.ShapeDtypeStruct((nb * t_new, DSA_HEADS * HEAD_DIM), F32),
        grid_spec=grid_spec,
        compiler_params=_cparams(("parallel", "arbitrary")),
        name="sample_dsa",
    )(page_table, t5, z, scores, thr, kd_new, vd_new, *([kd_pool] * g_pages), *([vd_pool] * g_pages))


def _mem_attn_kernel(q_ref, k_ref, v_ref, o_ref):
    for s in range(q_ref.shape[0]):
        q = q_ref[s].astype(BF16)
        for h in range(MEM_HEADS):
            k = k_ref[s, :, h, :].astype(BF16)
            v = v_ref[s, :, h, :].astype(BF16)
            logits = _dot_nt(q[:, h * 128:(h + 1) * 128], k) * MEM_SCALE
            p = jnp.exp(logits - jnp.max(logits, axis=-1, keepdims=True))
            o = _dot(p.astype(BF16), v) / jnp.sum(p, axis=-1, keepdims=True)
            o_ref[s, :, h * 128:(h + 1) * 128] = o


def _mem_attn(z3, mem_k, mem_v, *, seqs_per_step, rows_per_step):
    groups, rows, _ = z3.shape
    n_mem = mem_k.shape[1]
    gs, rs = seqs_per_step, rows_per_step
    kv_spec = pl.BlockSpec((gs, n_mem, MEM_HEADS, HEAD_DIM), lambda a, b: (a, 0, 0, 0))
    return pl.pallas_call(
        _mem_attn_kernel,
        out_shape=jax.ShapeDtypeStruct((groups, rows, MEM_HEADS * HEAD_DIM), F32),
        grid=(groups // gs, rows // rs),
        in_specs=[pl.BlockSpec((gs, rs, 512), lambda a, b: (a, b, C_QM // 512)), kv_spec, kv_spec],
        out_specs=pl.BlockSpec((gs, rs, 512), lambda a, b: (a, b, 0)),
        compiler_params=_cparams(("parallel", "arbitrary")),
        name="mem_attn",
    )(z3, mem_k, mem_v)


def _silu(g):
    return g * (1.0 / (1.0 + jnp.exp(-g)))


def _merge_kernel(x_ref, omla_ref, od_ref, om_ref, gmla_ref, gdsa_ref, gmem_ref, wout_ref, fg_ref, y_ref):
    o = jnp.concatenate([(omla_ref[...] * _silu(gmla_ref[...])).astype(BF16),
                         (od_ref[...] * _silu(gdsa_ref[...])).astype(BF16),
                         (om_ref[...] * _silu(gmem_ref[...])).astype(BF16)], axis=-1)
    xo = x_ref[...] + _dot(o, wout_ref[...])
    y_ref[...] = _rms(xo, fg_ref[...])


def _merge(x, omla, od, om, z, wout, fg):
    n, d = x.shape
    tm = min(256, n)
    row = lambda w, c=0: pl.BlockSpec((tm, w), lambda i, c=c, w=w: (i, c // w))
    return pl.pallas_call(
        _merge_kernel,
        out_shape=jax.ShapeDtypeStruct((n, d), F32),
        grid=(n // tm,),
        in_specs=[row(d), row(1024), row(512), row(512), row(1024, C_GMLA), row(512, C_GDSA), row(512, C_GMEM),
                  pl.BlockSpec(wout.shape, lambda i: (0, 0)), pl.BlockSpec((1, d), lambda i: (0, 0))],
        out_specs=row(d),
        compiler_params=_cparams(("parallel",)),
        name="merge",
    )(x, omla, od, om, z, z, z, wout, fg)


def _rope_tables(pos):
    half = QK_ROPE // 2
    inv_freq = jnp.exp(-math.log(ROPE_THETA) * jnp.arange(half, dtype=F32) / half)
    ang = pos.astype(F32)[:, None] * inv_freq[None, :]
    cos, sin = jnp.cos(ang), jnp.sin(ang)
    return jnp.tile(cos, (1, 4)), jnp.concatenate([-sin, sin, -sin, sin], axis=1)


def _permute_w_in(w):
    d = w.shape[0]
    seg = lambda a, b: w[:, a:b]
    return jnp.concatenate([
        seg(0, 512), seg(512, 768), seg(768, 832), seg(3904, 3968), seg(3968, 3984), jnp.zeros((d, 112), w.dtype),
        seg(832, 1856), seg(1856, 2368), seg(2368, 2624), seg(2624, 2880), seg(2880, 3904), seg(3984, 4496),
        seg(4496, 5008), seg(5008, 5520)], axis=1).astype(BF16)


def kernel(x_prompt, x_sample, mem_prompt, cache_mla_ckv, cache_mla_krope, cache_dsa_k, cache_dsa_v, cache_idx_k,
           cache_mem_k, cache_mem_v, page_table, t5_bias, norm_g, w_in, qa_norm_g, w_uq, kva_norm_g, w_ukv,
           idx_norm_g, mem_norm_g, w_mem_kv, w_out, final_norm_g, *, tk=512, g_pages=8):
    bp, seq, d = x_prompt.shape
    nb, t_new, _ = x_sample.shape
    n_mem = mem_prompt.shape[1]
    npages = page_table.shape[1]
    past = npages * PAGE_SIZE
    depth = norm_g.shape[0]
    assert depth == 1
    l = 0
    tq = 128
    tk = min(tk, seq)
    g_pages = min(g_pages, npages)
    topk_p = min(TOPK_MAX, seq // 4)
    topk_s = min(TOPK_MAX, (past + t_new) // 4)

    w_in_p = _permute_w_in(w_in[l])
    wuq = jnp.concatenate([w_uq[l][:, :, :HEAD_DIM].reshape(Q_LORA, -1),
                           w_uq[l][:, :, HEAD_DIM:].reshape(Q_LORA, -1)], axis=1).astype(BF16)
    wk = jnp.transpose(w_ukv[l][:, :, :HEAD_DIM], (1, 2, 0)).astype(BF16)
    wv = jnp.transpose(w_ukv[l][:, :, HEAD_DIM:], (1, 0, 2)).astype(BF16)
    wout = w_out[l].astype(BF16)
    wmem = w_mem_kv[l].astype(BF16)
    g_row = lambda g: g.reshape(1, -1)
    idxg128 = jnp.concatenate([jnp.zeros((1, 64), F32), g_row(idx_norm_g[l])], axis=1)
    fg = g_row(final_norm_g)

    def branch(x2, cos_t, sin_t, q_dtype):
        z = _in_proj(x2, g_row(norm_g[l]), w_in_p)
        outs = _post(z, cos_t, sin_t, g_row(qa_norm_g[l]), g_row(kva_norm_g[l]), idxg128, wuq, wk, q_dtype)
        return z, outs

    xp2 = x_prompt.reshape(bp * seq, d)
    cos_p, sin_p = _rope_tables(jnp.arange(seq, dtype=I32))
    zp, (ckv_p, krope_p, ki_p, kd_p, vd_p, kcat_p, kip0_p, kip1_p, kdb_p, vdb_p, qcat_p) = branch(xp2, cos_p, sin_p, BF16)
    omla_p, od_p = _prompt_attn(t5_bias, qcat_p, zp, kcat_p, kip0_p, kip1_p, kdb_p, vdb_p, wv,
                                batch=bp, seq=seq, tq=tq, tk=tk, topk=topk_p)
    zmem = _in_proj(mem_prompt.reshape(bp * n_mem, d), g_row(mem_norm_g[l]), wmem)
    mk_p = zmem[:, :MEM_HEADS * HEAD_DIM].reshape(bp, n_mem, MEM_HEADS, HEAD_DIM)
    mv_p = zmem[:, MEM_HEADS * HEAD_DIM:].reshape(bp, n_mem, MEM_HEADS, HEAD_DIM)
    om_p = _mem_attn(zp.reshape(bp, seq, D_Z), mk_p, mv_p, seqs_per_step=1, rows_per_step=min(512, seq))
    y_prompt = _merge(xp2, omla_p, od_p, om_p.reshape(bp * seq, -1), zp, wout, fg).reshape(bp, seq, d)

    xs2 = x_sample.reshape(nb * t_new, d)
    tm_s = min(256, nb * t_new)
    pos_s = past + (jnp.arange(tm_s, dtype=I32) % t_new)
    cos_s, sin_s = _rope_tables(pos_s)
    zs, (ckv_s, krope_s, ki_s, kd_s, vd_s, _, _, _, _, _, qcat_s) = branch(xs2, cos_s, sin_s, F32)
    wi_s = zs[:, C_WI:C_WI + IDX_HEADS].reshape(nb, t_new, IDX_HEADS // 2, 2)
    rows_i = (IDX_HEADS // 2) * t_new
    w_even = jnp.transpose(wi_s[..., 0], (0, 2, 1)).reshape(nb, rows_i, 1)
    w_odd = jnp.transpose(wi_s[..., 1], (0, 2, 1)).reshape(nb, rows_i, 1)
    qi_eo = jnp.transpose(zs[:, C_QI:C_QI + IDX_HEADS * IDX_DIM].reshape(nb, t_new, IDX_HEADS // 2, 2, IDX_DIM),
                          (0, 3, 2, 1, 4)).reshape(nb, 2, rows_i, IDX_DIM)
    scores, omla_s = _sample_mla(page_table, qcat_s, qi_eo, w_even, w_odd, ckv_s, krope_s, ki_s, wv,
                                 cache_mla_ckv, cache_mla_krope, cache_idx_k, nb=nb, t_new=t_new, g_pages=g_pages)
    thr = _sample_thr(scores, topk=topk_s)
    od_s = _sample_dsa(page_table, t5_bias, zs, scores, thr, kd_s, vd_s, cache_dsa_k, cache_dsa_v,
                       nb=nb, t_new=t_new, g_pages=g_pages)
    om_s = _mem_attn(zs.reshape(nb, t_new, D_Z), cache_mem_k[l], cache_mem_v[l],
                     seqs_per_step=min(4, nb), rows_per_step=t_new)
    y_sample = _merge(xs2, omla_s, od_s, om_s.reshape(nb * t_new, -1), zs, wout, fg).reshape(nb, t_new, d)

    st = lambda a, *shape: a.reshape((1,) + shape)
    return (y_prompt, y_sample,
            st(ckv_p, bp, seq, KV_LORA), st(krope_p, bp, seq, QK_ROPE),
            st(kd_p, bp, seq, DSA_KV_HEADS, HEAD_DIM), st(vd_p, bp, seq, DSA_KV_HEADS, HEAD_DIM),
            st(ki_p, bp, seq, IDX_DIM), st(mk_p, bp, n_mem, MEM_HEADS, HEAD_DIM), st(mv_p, bp, n_mem, MEM_HEADS, HEAD_DIM),
            st(ckv_s, nb, t_new, KV_LORA), st(krope_s, nb, t_new, QK_ROPE),
            st(kd_s, nb, t_new, DSA_KV_HEADS, HEAD_DIM), st(vd_s, nb, t_new, DSA_KV_HEADS, HEAD_DIM),
            st(ki_s, nb, t_new, IDX_DIM))
```

```python
import functools
import math

import numpy as np
import jax
import jax.numpy as jnp
from jax import lax
from jax.experimental import pallas as pl
from jax.experimental.pallas import tpu as pltpu

F32 = jnp.float32
BF16 = jnp.bfloat16
I32 = jnp.int32

HEAD_DIM = 128
MLA_HEADS = 8
DSA_HEADS = 4
DSA_KV_HEADS = 2
MEM_HEADS = 4
Q_LORA = 512
KV_LORA = 256
QK_ROPE = 64
IDX_HEADS = 16
IDX_DIM = 64
TOPK_MAX = 256
PAGE_SIZE = 128
NUM_BUCKETS = 32
ROPE_THETA = 10000.0
MLA_SCALE = (HEAD_DIM + QK_ROPE) ** -0.5
DSA_SCALE = HEAD_DIM ** -0.5
MEM_SCALE = HEAD_DIM ** -0.5
IDX_W_SCALE = (IDX_DIM ** -0.5) * (IDX_HEADS ** -0.5)
EPS = 1e-6
NEG_INF = float("-inf")
F32_MAX = float(np.finfo(np.float32).max)

C_Q, C_KV, C_KRKI, C_WI, C_GMLA, C_QD, C_KD, C_VD, C_QI, C_GDSA, C_QM, C_GMEM, D_Z = (
    0, 512, 768, 896, 1024, 2048, 2560, 2816, 3072, 4096, 4608, 5120, 5632)

T5_BUCKET_START = tuple(range(17)) + (19, 21, 24, 27, 31, 35, 40, 46, 52, 59, 67, 77, 87, 99, 113)
T5_FAR = T5_BUCKET_START[-1]

VMEM_LIMIT = 56 * 1024 * 1024


def _cparams(sem):
    return pltpu.CompilerParams(dimension_semantics=sem, vmem_limit_bytes=VMEM_LIMIT)


def _dot_nt(a, b):
    return lax.dot_general(a, b, (((1,), (1,)), ((), ())), preferred_element_type=F32)


def _dot(a, b):
    return jnp.dot(a, b, preferred_element_type=F32)


def _rms(x, g):
    return x * lax.rsqrt(jnp.mean(x * x, axis=-1, keepdims=True) + EPS) * g


def _inproj_kernel(x_ref, g_ref, w_ref, z_ref, h_ref):
    @pl.when(pl.program_id(1) == 0)
    def _():
        h_ref[...] = _rms(x_ref[...], g_ref[...]).astype(BF16)

    z_ref[...] = _dot(h_ref[...], w_ref[...])


def _in_proj(x, g, w, tn=512):
    n, d = x.shape
    m = w.shape[1]
    tm = min(512, n)
    return pl.pallas_call(
        _inproj_kernel,
        out_shape=jax.ShapeDtypeStruct((n, m), F32),
        grid=(n // tm, m // tn),
        in_specs=[pl.BlockSpec((tm, d), lambda i, j: (i, 0)),
                  pl.BlockSpec((1, d), lambda i, j: (0, 0)),
                  pl.BlockSpec((d, tn), lambda i, j: (0, j))],
        out_specs=pl.BlockSpec((tm, tn), lambda i, j: (i, j)),
        scratch_shapes=[pltpu.VMEM((tm, d), BF16)],
        compiler_params=_cparams(("parallel", "arbitrary")),
        name="in_proj",
    )(x, g, w)


def _post_kernel(cq_ref, ckv_ref, krki_ref, kd_ref, vd_ref, cos_ref, sin_ref, qag_ref, kvag_ref, idxg_ref,
                 wuq_ref, wk_ref,
                 ckv_o, krope_o, ki_o, kd_o, vd_o, kcat_o, kip0_o, kip1_o, kdb_o, vdb_o, qcat_o):
    tm = cq_ref.shape[0]
    lane = lax.broadcasted_iota(I32, (tm, 128), 1)
    hi = lane >= 64
    second_half = (lane % 64) >= 32
    cos_t = cos_ref[...]
    sin_t = sin_ref[...]

    def rope128(x):
        swapped = jnp.where(second_half, pltpu.roll(x, 32, 1), pltpu.roll(x, 96, 1))
        return x * cos_t + swapped * sin_t

    ckv_n = _rms(ckv_ref[...], kvag_ref[...])
    ckv_o[...] = ckv_n

    krki = krki_ref[...]
    rot = rope128(krki)
    krope_o[...] = rot[:, :QK_ROPE]
    ms = jnp.sum(jnp.where(hi, krki * krki, 0.0), axis=-1, keepdims=True) * (1.0 / IDX_DIM)
    ki_hi = krki * lax.rsqrt(ms + EPS) * idxg_ref[...]
    ki_lo = pltpu.roll(ki_hi, 64, 1)
    ki_o[...] = ki_lo[:, :IDX_DIM]
    kip0_o[...] = jnp.where(hi, 0.0, ki_lo).astype(BF16)
    kip1_o[...] = jnp.where(hi, ki_hi, 0.0).astype(BF16)
    kcat_o[:, :KV_LORA] = ckv_n.astype(BF16)
    kcat_o[:, KV_LORA:] = jnp.where(hi, 0.0, rot).astype(BF16)

    kd = kd_ref[...]
    vd = vd_ref[...]
    kd_o[...] = kd
    vd_o[...] = vd
    kdb_o[...] = kd.astype(BF16)
    vdb_o[...] = vd.astype(BF16)

    cqn = _rms(cq_ref[...], qag_ref[...]).astype(BF16)
    qall = _dot(cqn, wuq_ref[...])
    nope_w = MLA_HEADS * HEAD_DIM
    for h in range(MLA_HEADS):
        qn = qall[:, h * HEAD_DIM:(h + 1) * HEAD_DIM].astype(BF16)
        qcat_o[h, :, :KV_LORA] = _dot(qn, wk_ref[h]).astype(qcat_o.dtype)
    for c in range(MLA_HEADS // 2):
        rc = rope128(qall[:, nope_w + 128 * c:nope_w + 128 * (c + 1)])
        qcat_o[2 * c, :, KV_LORA:] = jnp.where(hi, 0.0, rc).astype(qcat_o.dtype)
        qcat_o[2 * c + 1, :, KV_LORA:] = jnp.where(hi, 0.0, pltpu.roll(rc, 64, 1)).astype(qcat_o.dtype)


def _post(z, cos_t, sin_t, qag, kvag, idxg128, wuq, wk, q_dtype):
    n = z.shape[0]
    tm = min(256, n)
    nt = cos_t.shape[0] // tm
    row = lambda c, w: pl.BlockSpec((tm, w), lambda i, c=c, w=w: (i, c // w))
    full2 = lambda a: pl.BlockSpec(a.shape, lambda i: (0, 0))
    full3 = lambda a: pl.BlockSpec(a.shape, lambda i: (0, 0, 0))
    tab = pl.BlockSpec((tm, 128), lambda i: (i % nt, 0))
    out_shapes = [
        jax.ShapeDtypeStruct((n, KV_LORA), F32), jax.ShapeDtypeStruct((n, QK_ROPE), F32),
        jax.ShapeDtypeStruct((n, IDX_DIM), F32), jax.ShapeDtypeStruct((n, 256), F32),
        jax.ShapeDtypeStruct((n, 256), F32), jax.ShapeDtypeStruct((n, 384), BF16),
        jax.ShapeDtypeStruct((n, 128), BF16), jax.ShapeDtypeStruct((n, 128), BF16),
        jax.ShapeDtypeStruct((n, 256), BF16), jax.ShapeDtypeStruct((n, 256), BF16),
        jax.ShapeDtypeStruct((MLA_HEADS, n, 384), q_dtype)]
    o2 = lambda w: pl.BlockSpec((tm, w), lambda i: (i, 0))
    out_specs = [o2(KV_LORA), o2(QK_ROPE), o2(IDX_DIM), o2(256), o2(256), o2(384), o2(128), o2(128), o2(256),
                 o2(256), pl.BlockSpec((MLA_HEADS, tm, 384), lambda i: (0, i, 0))]
    return pl.pallas_call(
        _post_kernel,
        out_shape=out_shapes,
        grid=(n // tm,),
        in_specs=[row(C_Q, 512), row(C_KV, 256), row(C_KRKI, 128), row(C_KD, 256), row(C_VD, 256), tab, tab,
                  full2(qag), full2(kvag), full2(idxg128), full2(wuq), full3(wk)],
        out_specs=out_specs,
        compiler_params=_cparams(("parallel",)),
        name="post",
    )(z, z, z, z, z, cos_t, sin_t, qag, kvag, idxg128, wuq, wk)


def _kth_largest(count_ge, k, rows):
    sign = np.int32(-2 ** 31)
    key_neg_inf = np.int32(-2139095041)

    def to_float(u):
        key = u ^ sign
        bits = jnp.where(key >= 0, key, key ^ np.int32(0x7FFFFFFF))
        return key, lax.bitcast_convert_type(bits, F32)

    def body(b, prefix):
        cand = prefix | jnp.left_shift(np.int32(1), jnp.asarray(31 - b, I32))
        key, cand_f = to_float(cand)
        ok = (count_ge(cand_f) >= k) | (key <= key_neg_inf)
        return jnp.where(ok, cand, prefix)

    prefix = lax.fori_loop(0, 32, body, jnp.zeros((rows, 1), I32))
    return to_float(prefix)[1]


def _t5_bias_tile(dist, t5_ref, heads):
    ge = [dist >= T5_BUCKET_START[b] for b in range(1, NUM_BUCKETS)]
    out = []
    for h in heads:
        bias = jnp.full(dist.shape, t5_ref[0, h], F32)
        for b in range(1, NUM_BUCKETS):
            bias = jnp.where(ge[b - 1], t5_ref[b, h], bias)
        out.append(bias)
    return out


def _softmax_update(s, m, l):
    m_new = jnp.maximum(m, jnp.max(s, axis=-1, keepdims=True))
    m_use = jnp.where(m_new == NEG_INF, 0.0, m_new)
    alpha = jnp.exp(m - m_use)
    p = jnp.exp(s - m_use)
    return m_new, alpha * l + jnp.sum(p, axis=-1, keepdims=True), alpha, p


def _prompt_attn_kernel(t5_ref, qcat_ref, qd_ref, qi_ref, wi_ref, kcat_ref, kip0_ref, kip1_ref, kd_ref, vd_ref,
                        wv_ref, omla_ref, od_ref, sc_ref, acc_ref, accd_ref, band_ref, *, t, topk):
    i = pl.program_id(1)
    rows_m = MLA_HEADS * t
    row = lax.broadcasted_iota(I32, (t, t), 0)
    col = lax.broadcasted_iota(I32, (t, t), 1)
    vis = col <= row

    @pl.when((pl.program_id(0) == 0) & (i == 0))
    def _():
        left = _t5_bias_tile(row - col + t, t5_ref, range(DSA_HEADS))
        diag = _t5_bias_tile(row - col, t5_ref, range(DSA_HEADS))
        for h in range(DSA_HEADS):
            band_ref[h, :, :t] = left[h]
            band_ref[h, :, t:] = diag[h]

    qall = qcat_ref[...].reshape(rows_m, 384)
    qi_b = qi_ref[...].astype(BF16)
    wsc = wi_ref[...] * IDX_W_SCALE
    acc_ref[...] = jnp.zeros_like(acc_ref)

    def kv_step(j, carry, masked):
        m, l = carry
        start = pl.multiple_of(j * t, t)
        kc = kcat_ref[pl.ds(start, t), :]
        s = _dot_nt(qall, kc) * MLA_SCALE
        if masked:
            s = jnp.where(vis[None], s.reshape(MLA_HEADS, t, t), NEG_INF).reshape(rows_m, t)
        m_new, l_new, alpha, p = _softmax_update(s, m, l)
        acc_ref[...] = acc_ref[...] * alpha + _dot(p.astype(BF16), kc[:, :KV_LORA])
        kp = (kip0_ref[pl.ds(start, t), :], kip1_ref[pl.ds(start, t), :])
        sc = jnp.zeros((t, t), F32)
        for h in range(IDX_HEADS):
            d = _dot_nt(qi_b[:, 128 * (h // 2):128 * (h // 2 + 1)], kp[h % 2])
            sc = sc + jnp.maximum(d, 0.0) * wsc[:, h:h + 1]
        if masked:
            sc = jnp.where(vis, sc, NEG_INF)
        sc_ref[j] = sc
        return m_new, l_new

    init = (jnp.full((rows_m, 1), NEG_INF, F32), jnp.zeros((rows_m, 1), F32))
    carry = lax.fori_loop(0, i, functools.partial(kv_step, masked=False), init)
    m_mla, l_mla = kv_step(i, carry, masked=True)

    def count_ge(cand):
        def body(j, c):
            x = sc_ref[j]
            for u in range(t // 128):
                c = c + jnp.where(x[:, u * 128:(u + 1) * 128] >= cand, 1, 0)
            return c
        c = lax.fori_loop(0, i + 1, body, jnp.zeros((t, 128), I32))
        return jnp.sum(c, axis=-1, keepdims=True)

    thr = jnp.maximum(_kth_largest(count_ge, topk, t), -F32_MAX)

    qd_b = qd_ref[...].astype(BF16)
    qg = [jnp.concatenate([qd_b[:, (2 * g) * 128:(2 * g + 1) * 128],
                           qd_b[:, (2 * g + 1) * 128:(2 * g + 2) * 128]], axis=0) for g in range(DSA_KV_HEADS)]
    accd_ref[...] = jnp.zeros_like(accd_ref)

    def dsa_step(j, carry, kind):
        start = pl.multiple_of(j * t, t)
        sel = sc_ref[j] >= thr
        new = []
        for g in range(DSA_KV_HEADS):
            m, l = carry[g]
            kd = kd_ref[pl.ds(start, t), g * 128:(g + 1) * 128]
            vd = vd_ref[pl.ds(start, t), g * 128:(g + 1) * 128]
            s = _dot_nt(qg[g], kd) * DSA_SCALE
            parts = []
            for r in range(2):
                h = 2 * g + r
                if kind == "far":
                    b = t5_ref[NUM_BUCKETS - 1, h]
                elif kind == "left":
                    b = band_ref[h, :, :t]
                else:
                    b = band_ref[h, :, t:]
                parts.append(jnp.where(sel, s[r * t:(r + 1) * t] + b, NEG_INF))
            s = jnp.concatenate(parts, axis=0)
            m_new, l_new, alpha, p = _softmax_update(s, m, l)
            accd_ref[g] = accd_ref[g] * alpha + _dot(p.astype(BF16), vd)
            new.append((m_new, l_new))
        return tuple(new)

    init_d = tuple((jnp.full((2 * t, 1), NEG_INF, F32), jnp.zeros((2 * t, 1), F32)) for _ in range(DSA_KV_HEADS))
    carry_d = lax.fori_loop(0, jnp.maximum(i - 1, 0), functools.partial(dsa_step, kind="far"), init_d)
    carry_d = lax.fori_loop(jnp.maximum(i - 1, 0), i, functools.partial(dsa_step, kind="left"), carry_d)
    carry_d = dsa_step(i, carry_d, "diag")

    for g in range(DSA_KV_HEADS):
        o = accd_ref[g] / carry_d[g][1]
        for r in range(2):
            h = 2 * g + r
            od_ref[:, h * 128:(h + 1) * 128] = o[r * t:(r + 1) * t]
    lat = (acc_ref[...] / l_mla).astype(BF16)
    for h in range(MLA_HEADS):
        omla_ref[:, h * 128:(h + 1) * 128] = _dot(lat[h * t:(h + 1) * t], wv_ref[h])


def _prompt_attn(t5, qcat8, z, kcat, kip0, kip1, kdb, vdb, wv, *, batch, seq, t, topk):
    assert t >= T5_FAR and seq % t == 0
    n = batch * seq
    nq = seq // t
    qrow = lambda c, w: pl.BlockSpec((t, w), lambda b, i, c=c, w=w: (b * nq + i, c // w))
    kfull = lambda w: pl.BlockSpec((seq, w), lambda b, i: (b, 0))
    kern = functools.partial(_prompt_attn_kernel, t=t, topk=topk)
    return pl.pallas_call(
        kern,
        out_shape=[jax.ShapeDtypeStruct((n, MLA_HEADS * HEAD_DIM), F32),
                   jax.ShapeDtypeStruct((n, DSA_HEADS * HEAD_DIM), F32)],
        grid=(batch, nq),
        in_specs=[pl.BlockSpec(memory_space=pltpu.SMEM),
                  pl.BlockSpec((MLA_HEADS, t, 384), lambda b, i: (0, b * nq + i, 0)),
                  qrow(C_QD, 512), qrow(C_QI, 1024), qrow(C_WI, 128),
                  kfull(384), kfull(128), kfull(128), kfull(256), kfull(256),
                  pl.BlockSpec(wv.shape, lambda b, i: (0, 0, 0))],
        out_specs=[pl.BlockSpec((t, MLA_HEADS * HEAD_DIM), lambda b, i: (b * nq + i, 0)),
                   pl.BlockSpec((t, DSA_HEADS * HEAD_DIM), lambda b, i: (b * nq + i, 0))],
        scratch_shapes=[pltpu.VMEM((nq, t, t), F32),
                        pltpu.VMEM((MLA_HEADS * t, KV_LORA), F32),
                        pltpu.VMEM((DSA_KV_HEADS, 2 * t, HEAD_DIM), F32),
                        pltpu.VMEM((DSA_HEADS, t, 2 * t), F32)],
        compiler_params=_cparams(("arbitrary", "arbitrary")),
        name="prompt_attn",
    )(t5, qcat8, z, z, z, kcat, kip0, kip1, kdb, vdb, wv)


def _pad_rows(x, rows):
    return jnp.concatenate([x, jnp.zeros((rows - x.shape[0], x.shape[1]), x.dtype)], axis=0)


def _next_chunk(n, c, nch):
    last = c + 1 == nch
    return jnp.where(last, n + 1, n), jnp.where(last, 0, c + 1)


def _sample_mla_kernel(pt_ref, qcat_ref, qi_ref, we_ref, wo_ref, ckvn_ref, krn_ref, kin_ref, wv_ref,
                       ckv_hbm, krt_hbm, kit_hbm, sc_ref, omla_ref, ckv_buf, krt_buf, kit_buf, sem, *, c_pages):
    n = pl.program_id(0)
    nb = pl.num_programs(0)
    nch = pt_ref.shape[1] // c_pages
    cw = c_pages * PAGE_SIZE
    past = nch * cw
    t_new = ckvn_ref.shape[0]
    rows_m = MLA_HEADS * t_new

    def chunk_copies(seq, c, slot):
        cps = []
        for u in range(c_pages):
            page = pt_ref[seq, c * c_pages + u]
            rows = pl.ds(u * PAGE_SIZE, PAGE_SIZE)
            cps.append(pltpu.make_async_copy(ckv_hbm.at[0, page], ckv_buf.at[slot, rows, :], sem.at[0, slot]))
            cps.append(pltpu.make_async_copy(krt_hbm.at[0, page], krt_buf.at[slot, :, rows], sem.at[1, slot]))
            cps.append(pltpu.make_async_copy(kit_hbm.at[0, page], kit_buf.at[slot, :, rows], sem.at[2, slot]))
        return cps

    @pl.when(n == 0)
    def _():
        for cp in chunk_copies(0, 0, 0):
            cp.start()

    q = qcat_ref[...].reshape(rows_m, 384).astype(BF16)
    ql = q[:, :KV_LORA]
    qr = q[:, KV_LORA:KV_LORA + QK_ROPE]
    q_even = qi_ref[0].astype(BF16)
    q_odd = qi_ref[1].astype(BF16)
    w_even = we_ref[...] * IDX_W_SCALE
    w_odd = wo_ref[...] * IDX_W_SCALE

    def idx_scores(d_even, d_odd):
        d = jnp.maximum(d_even, 0.0) * w_even + jnp.maximum(d_odd, 0.0) * w_odd
        return jnp.sum(d.reshape(IDX_HEADS // 2, t_new, d.shape[-1]), axis=0)

    def chunk(c, carry):
        m, l, acc = carry
        slot = c % 2
        seq2, c2 = _next_chunk(n, c, nch)

        @pl.when(seq2 < nb)
        def _():
            for cp in chunk_copies(seq2, c2, 1 - slot):
                cp.start()

        for cp in chunk_copies(n, c, slot):
            cp.wait()
        ck = ckv_buf[slot].astype(BF16)
        s = (_dot_nt(ql, ck) + _dot(qr, krt_buf[slot].astype(BF16))) * MLA_SCALE
        m_new, l_new, alpha, p = _softmax_update(s, m, l)
        acc = acc * alpha + _dot(p.astype(BF16), ck)
        kit = kit_buf[slot].astype(BF16)
        sc_ref[:, pl.ds(pl.multiple_of(c * cw, cw), cw)] = idx_scores(_dot(q_even, kit), _dot(q_odd, kit))
        return m_new, l_new, acc

    init = (jnp.full((rows_m, 1), NEG_INF, F32), jnp.zeros((rows_m, 1), F32), jnp.zeros((rows_m, KV_LORA), F32))
    m, l, acc = lax.fori_loop(0, nch, chunk, init)

    tok = lax.broadcasted_iota(I32, (t_new, PAGE_SIZE), 0)
    col = lax.broadcasted_iota(I32, (t_new, PAGE_SIZE), 1)
    vis = col <= tok
    ck = _pad_rows(ckvn_ref[...], PAGE_SIZE).astype(BF16)
    kr = _pad_rows(krn_ref[...], PAGE_SIZE).astype(BF16)
    ki = _pad_rows(kin_ref[...], PAGE_SIZE).astype(BF16)
    s = (_dot_nt(ql, ck) + _dot_nt(qr, kr)) * MLA_SCALE
    s = jnp.where(vis[None], s.reshape(MLA_HEADS, t_new, PAGE_SIZE), NEG_INF).reshape(rows_m, PAGE_SIZE)
    m, l, alpha, p = _softmax_update(s, m, l)
    acc = acc * alpha + _dot(p.astype(BF16), ck)
    sc_ref[:, past:] = jnp.where(vis, idx_scores(_dot_nt(q_even, ki), _dot_nt(q_odd, ki)), NEG_INF)
    lat = (acc / l).astype(BF16)
    for h in range(MLA_HEADS):
        omla_ref[:, h * 128:(h + 1) * 128] = _dot(lat[h * t_new:(h + 1) * t_new], wv_ref[h])


def _sample_mla(page_table, qcat8, qi_eo, w_even, w_odd, ckv_new, kr_new, ki_new, wv, ckv_pool, krt_pool, kit_pool,
                *, nb, t_new, c_pages):
    npages = page_table.shape[1]
    nch = npages // c_pages
    assert nch * c_pages == npages and nch % 2 == 0
    cw = c_pages * PAGE_SIZE
    width = npages * PAGE_SIZE + PAGE_SIZE
    rows_i = (IDX_HEADS // 2) * t_new
    any_spec = pl.BlockSpec(memory_space=pl.ANY)
    in_specs = [pl.BlockSpec((MLA_HEADS, t_new, 384), lambda n, pt: (0, n, 0)),
                pl.BlockSpec((None, 2, rows_i, IDX_DIM), lambda n, pt: (n, 0, 0, 0)),
                pl.BlockSpec((None, rows_i, 1), lambda n, pt: (n, 0, 0)),
                pl.BlockSpec((None, rows_i, 1), lambda n, pt: (n, 0, 0)),
                pl.BlockSpec((t_new, KV_LORA), lambda n, pt: (n, 0)),
                pl.BlockSpec((t_new, QK_ROPE), lambda n, pt: (n, 0)),
                pl.BlockSpec((t_new, IDX_DIM), lambda n, pt: (n, 0)),
                pl.BlockSpec(wv.shape, lambda n, pt: (0, 0, 0)),
                any_spec, any_spec, any_spec]
    grid_spec = pltpu.PrefetchScalarGridSpec(
        num_scalar_prefetch=1, grid=(nb,), in_specs=in_specs,
        out_specs=[pl.BlockSpec((None, t_new, width), lambda n, pt: (n, 0, 0)),
                   pl.BlockSpec((t_new, MLA_HEADS * HEAD_DIM), lambda n, pt: (n, 0))],
        scratch_shapes=[pltpu.VMEM((2, cw, KV_LORA), F32), pltpu.VMEM((2, QK_ROPE, cw), F32),
                        pltpu.VMEM((2, IDX_DIM, cw), F32), pltpu.SemaphoreType.DMA((3, 2))])
    return pl.pallas_call(
        functools.partial(_sample_mla_kernel, c_pages=c_pages),
        out_shape=[jax.ShapeDtypeStruct((nb, t_new, width), F32),
                   jax.ShapeDtypeStruct((nb * t_new, MLA_HEADS * HEAD_DIM), F32)],
        grid_spec=grid_spec,
        compiler_params=_cparams(("arbitrary",)),
        name="sample_mla",
    )(page_table, qcat8, qi_eo, w_even, w_odd, ckv_new, kr_new, ki_new, wv, ckv_pool, krt_pool, kit_pool)


def _sample_thr_kernel(sc_ref, thr_ref, *, topk, chunk):
    nseq, t_new, width = sc_ref.shape
    rows = nseq * t_new

    def count_ge(cand):
        def body(c_idx, c):
            x = sc_ref[:, :, pl.ds(pl.multiple_of(c_idx * chunk, chunk), chunk)].reshape(rows, chunk)
            for u in range(chunk // 128):
                c = c + jnp.where(x[:, u * 128:(u + 1) * 128] >= cand, 1, 0)
            return c
        c = lax.fori_loop(0, width // chunk, body, jnp.zeros((rows, 128), I32))
        return jnp.sum(c, axis=-1, keepdims=True)

    thr_ref[...] = jnp.maximum(_kth_largest(count_ge, topk, rows), -F32_MAX)


def _sample_thr(scores, *, topk, nseq=8):
    nb, t_new, width = scores.shape
    nseq = min(nseq, nb)
    lanes = width // 128
    chunk = 128 * max(d for d in range(1, 5) if lanes % d == 0)
    return pl.pallas_call(
        functools.partial(_sample_thr_kernel, topk=topk, chunk=chunk),
        out_shape=jax.ShapeDtypeStruct((nb * t_new, 1), F32),
        grid=(nb // nseq,),
        in_specs=[pl.BlockSpec((nseq, t_new, width), lambda i: (i, 0, 0))],
        out_specs=pl.BlockSpec((nseq * t_new, 1), lambda i: (i, 0)),
        compiler_params=_cparams(("parallel",)),
        name="sample_thr",
    )(scores)


def _sample_dsa_kernel(pt_ref, t5_ref, qd_ref, sc_ref, thr_ref, kdn_ref, vdn_ref, kd_hbm, vd_hbm, od_ref,
                       kd_buf, vd_buf, sem, *, c_pages):
    n = pl.program_id(0)
    nb = pl.num_programs(0)
    nch = pt_ref.shape[1] // c_pages
    cw = c_pages * PAGE_SIZE
    past = nch * cw
    t_new = qd_ref.shape[0]

    def chunk_copies(seq, c, slot):
        cps = []
        for u in range(c_pages):
            page = pt_ref[seq, c * c_pages + u]
            rows = pl.ds(u * PAGE_SIZE, PAGE_SIZE)
            for g in range(DSA_KV_HEADS):
                cps.append(pltpu.make_async_copy(kd_hbm.at[0, page, :, g, :], kd_buf.at[slot, g, rows, :], sem.at[0, slot]))
                cps.append(pltpu.make_async_copy(vd_hbm.at[0, page, :, g, :], vd_buf.at[slot, g, rows, :], sem.at[1, slot]))
        return cps

    @pl.when(n == 0)
    def _():
        for cp in chunk_copies(0, 0, 0):
            cp.start()

    qd_b = qd_ref[...].astype(BF16)
    qg = [jnp.concatenate([qd_b[:, (2 * g) * 128:(2 * g + 1) * 128],
                           qd_b[:, (2 * g + 1) * 128:(2 * g + 2) * 128]], axis=0) for g in range(DSA_KV_HEADS)]
    thr = thr_ref[...]

    def attend(carry, k_of, v_of, sel, dist):
        if dist is not None:
            bias = _t5_bias_tile(dist, t5_ref, range(DSA_HEADS))
        new = []
        for g in range(DSA_KV_HEADS):
            m, l, acc = carry[g]
            s = _dot_nt(qg[g], k_of(g)) * DSA_SCALE
            parts = []
            for r in range(2):
                h = 2 * g + r
                b = bias[h] if dist is not None else t5_ref[NUM_BUCKETS - 1, h]
                parts.append(jnp.where(sel, s[r * t_new:(r + 1) * t_new] + b, NEG_INF))
            m_new, l_new, alpha, p = _softmax_update(jnp.concatenate(parts, axis=0), m, l)
            new.append((m_new, l_new, acc * alpha + _dot(p.astype(BF16), v_of(g))))
        return tuple(new)

    def chunk(c, carry, near):
        slot = c % 2
        seq2, c2 = _next_chunk(n, c, nch)

        @pl.when(seq2 < nb)
        def _():
            for cp in chunk_copies(seq2, c2, 1 - slot):
                cp.start()

        for cp in chunk_copies(n, c, slot):
            cp.wait()
        start = pl.multiple_of(c * cw, cw)
        sel = sc_ref[:, pl.ds(start, cw)] >= thr
        dist = None
        if near:
            tok = lax.broadcasted_iota(I32, (t_new, cw), 0)
            col = lax.broadcasted_iota(I32, (t_new, cw), 1)
            dist = (past + tok) - (start + col)
        return attend(carry, lambda g: kd_buf[slot, g].astype(BF16), lambda g: vd_buf[slot, g].astype(BF16), sel, dist)

    init = tuple((jnp.full((2 * t_new, 1), NEG_INF, F32), jnp.zeros((2 * t_new, 1), F32),
                  jnp.zeros((2 * t_new, HEAD_DIM), F32)) for _ in range(DSA_KV_HEADS))
    carry = lax.fori_loop(0, nch - 1, functools.partial(chunk, near=False), init)
    carry = chunk(nch - 1, carry, near=True)

    tok = lax.broadcasted_iota(I32, (t_new, PAGE_SIZE), 0)
    col = lax.broadcasted_iota(I32, (t_new, PAGE_SIZE), 1)
    kdn = _pad_rows(kdn_ref[...], PAGE_SIZE).astype(BF16)
    vdn = _pad_rows(vdn_ref[...], PAGE_SIZE).astype(BF16)
    carry = attend(carry, lambda g: kdn[:, g * 128:(g + 1) * 128], lambda g: vdn[:, g * 128:(g + 1) * 128],
                   sc_ref[:, past:] >= thr, tok - col)
    for g in range(DSA_KV_HEADS):
        o = carry[g][2] / carry[g][1]
        for r in range(2):
            h = 2 * g + r
            od_ref[:, h * 128:(h + 1) * 128] = o[r * t_new:(r + 1) * t_new]


def _sample_dsa(page_table, t5, z, scores, thr, kd_new, vd_new, kd_pool, vd_pool, *, nb, t_new, c_pages):
    npages = page_table.shape[1]
    nch = npages // c_pages
    assert nch * c_pages == npages and nch % 2 == 0 and c_pages * PAGE_SIZE >= T5_FAR
    cw = c_pages * PAGE_SIZE
    width = scores.shape[-1]
    any_spec = pl.BlockSpec(memory_space=pl.ANY)
    in_specs = [pl.BlockSpec(memory_space=pltpu.SMEM),
                pl.BlockSpec((t_new, 512), lambda n, pt: (n, C_QD // 512)),
                pl.BlockSpec((None, t_new, width), lambda n, pt: (n, 0, 0)),
                pl.BlockSpec((t_new, 1), lambda n, pt: (n, 0)),
                pl.BlockSpec((t_new, 256), lambda n, pt: (n, 0)),
                pl.BlockSpec((t_new, 256), lambda n, pt: (n, 0)),
                any_spec, any_spec]
    grid_spec = pltpu.PrefetchScalarGridSpec(
        num_scalar_prefetch=1, grid=(nb,), in_specs=in_specs,
        out_specs=pl.BlockSpec((t_new, DSA_HEADS * HEAD_DIM), lambda n, pt: (n, 0)),
        scratch_shapes=[pltpu.VMEM((2, DSA_KV_HEADS, cw, HEAD_DIM), F32), pltpu.VMEM((2, DSA_KV_HEADS, cw, HEAD_DIM), F32),
                        pltpu.SemaphoreType.DMA((2, 2))])
    return pl.pallas_call(
        functools.partial(_sample_dsa_kernel, c_pages=c_pages),
        out_shape=jax.ShapeDtypeStruct((nb * t_new, DSA_HEADS * HEAD_DIM), F32),
        grid_spec=grid_spec,
        compiler_params=_cparams(("arbitrary",)),
        name="sample_dsa",
    )(page_table, t5, z, scores, thr, kd_new, vd_new, kd_pool, vd_pool)


def _mem_attn_kernel(q_ref, k_ref, v_ref, o_ref):
    for s in range(q_ref.shape[0]):
        q = q_ref[s].astype(BF16)
        for h in range(MEM_HEADS):
            k = k_ref[s, :, h, :].astype(BF16)
            v = v_ref[s, :, h, :].astype(BF16)
            logits = _dot_nt(q[:, h * 128:(h + 1) * 128], k) * MEM_SCALE
            p = jnp.exp(logits - jnp.max(logits, axis=-1, keepdims=True))
            o = _dot(p.astype(BF16), v) / jnp.sum(p, axis=-1, keepdims=True)
            o_ref[s, :, h * 128:(h + 1) * 128] = o


def _mem_attn(z3, mem_k, mem_v, *, seqs_per_step, rows_per_step):
    groups, rows, _ = z3.shape
    n_mem = mem_k.shape[1]
    gs, rs = seqs_per_step, rows_per_step
    kv_spec = pl.BlockSpec((gs, n_mem, MEM_HEADS, HEAD_DIM), lambda a, b: (a, 0, 0, 0))
    return pl.pallas_call(
        _mem_attn_kernel,
        out_shape=jax.ShapeDtypeStruct((groups, rows, MEM_HEADS * HEAD_DIM), F32),
        grid=(groups // gs, rows // rs),
        in_specs=[pl.BlockSpec((gs, rs, 512), lambda a, b: (a, b, C_QM // 512)), kv_spec, kv_spec],
        out_specs=pl.BlockSpec((gs, rs, 512), lambda a, b: (a, b, 0)),
        compiler_params=_cparams(("parallel", "arbitrary")),
        name="mem_attn",
    )(z3, mem_k, mem_v)


def _silu(g):
    return g * (1.0 / (1.0 + jnp.exp(-g)))


def _merge_kernel(x_ref, omla_ref, od_ref, om_ref, gmla_ref, gdsa_ref, gmem_ref, wout_ref, fg_ref, y_ref):
    o = jnp.concatenate([(omla_ref[...] * _silu(gmla_ref[...])).astype(BF16),
                         (od_ref[...] * _silu(gdsa_ref[...])).astype(BF16),
                         (om_ref[...] * _silu(gmem_ref[...])).astype(BF16)], axis=-1)
    xo = x_ref[...] + _dot(o, wout_ref[...])
    y_ref[...] = _rms(xo, fg_ref[...])


def _merge(x, omla, od, om, z, wout, fg):
    n, d = x.shape
    tm = min(256, n)
    row = lambda w, c=0: pl.BlockSpec((tm, w), lambda i, c=c, w=w: (i, c // w))
    return pl.pallas_call(
        _merge_kernel,
        out_shape=jax.ShapeDtypeStruct((n, d), F32),
        grid=(n // tm,),
        in_specs=[row(d), row(1024), row(512), row(512), row(1024, C_GMLA), row(512, C_GDSA), row(512, C_GMEM),
                  pl.BlockSpec(wout.shape, lambda i: (0, 0)), pl.BlockSpec((1, d), lambda i: (0, 0))],
        out_specs=row(d),
        compiler_params=_cparams(("parallel",)),
        name="merge",
    )(x, omla, od, om, z, z, z, wout, fg)


def _rope_tables(pos):
    half = QK_ROPE // 2
    inv_freq = jnp.exp(-math.log(ROPE_THETA) * jnp.arange(half, dtype=F32) / half)
    ang = pos.astype(F32)[:, None] * inv_freq[None, :]
    cos, sin = jnp.cos(ang), jnp.sin(ang)
    return jnp.tile(cos, (1, 4)), jnp.concatenate([-sin, sin, -sin, sin], axis=1)


def _permute_w_in(w):
    d = w.shape[0]
    seg = lambda a, b: w[:, a:b]
    return jnp.concatenate([
        seg(0, 512), seg(512, 768), seg(768, 832), seg(3904, 3968), seg(3968, 3984), jnp.zeros((d, 112), w.dtype),
        seg(832, 1856), seg(1856, 2368), seg(2368, 2624), seg(2624, 2880), seg(2880, 3904), seg(3984, 4496),
        seg(4496, 5008), seg(5008, 5520)], axis=1).astype(BF16)


def kernel(x_prompt, x_sample, mem_prompt, cache_mla_ckv, cache_mla_krope, cache_dsa_k, cache_dsa_v, cache_idx_k,
           cache_mem_k, cache_mem_v, page_table, t5_bias, norm_g, w_in, qa_norm_g, w_uq, kva_norm_g, w_ukv,
           idx_norm_g, mem_norm_g, w_mem_kv, w_out, final_norm_g, *, t_blk=256, c_pages=16):
    bp, seq, d = x_prompt.shape
    nb, t_new, _ = x_sample.shape
    n_mem = mem_prompt.shape[1]
    npages = page_table.shape[1]
    past = npages * PAGE_SIZE
    depth = norm_g.shape[0]
    assert depth == 1
    l = 0
    t_blk = min(t_blk, seq)
    c_pages = min(c_pages, npages // 2)
    topk_p = min(TOPK_MAX, seq // 4)
    topk_s = min(TOPK_MAX, (past + t_new) // 4)

    w_in_p = _permute_w_in(w_in[l])
    wuq = jnp.concatenate([w_uq[l][:, :, :HEAD_DIM].reshape(Q_LORA, -1),
                           w_uq[l][:, :, HEAD_DIM:].reshape(Q_LORA, -1)], axis=1).astype(BF16)
    wk = jnp.transpose(w_ukv[l][:, :, :HEAD_DIM], (1, 2, 0)).astype(BF16)
    wv = jnp.transpose(w_ukv[l][:, :, HEAD_DIM:], (1, 0, 2)).astype(BF16)
    wout = w_out[l].astype(BF16)
    wmem = w_mem_kv[l].astype(BF16)
    g_row = lambda g: g.reshape(1, -1)
    idxg128 = jnp.concatenate([jnp.zeros((1, 64), F32), g_row(idx_norm_g[l])], axis=1)
    fg = g_row(final_norm_g)

    def branch(x2, cos_t, sin_t, q_dtype):
        z = _in_proj(x2, g_row(norm_g[l]), w_in_p)
        outs = _post(z, cos_t, sin_t, g_row(qa_norm_g[l]), g_row(kva_norm_g[l]), idxg128, wuq, wk, q_dtype)
        return z, outs

    xp2 = x_prompt.reshape(bp * seq, d)
    cos_p, sin_p = _rope_tables(jnp.arange(seq, dtype=I32))
    zp, (ckv_p, krope_p, ki_p, kd_p, vd_p, kcat_p, kip0_p, kip1_p, kdb_p, vdb_p, qcat_p) = branch(xp2, cos_p, sin_p, BF16)
    omla_p, od_p = _prompt_attn(t5_bias, qcat_p, zp, kcat_p, kip0_p, kip1_p, kdb_p, vdb_p, wv,
                                batch=bp, seq=seq, t=t_blk, topk=topk_p)
    zmem = _in_proj(mem_prompt.reshape(bp * n_mem, d), g_row(mem_norm_g[l]), wmem)
    mk_p = zmem[:, :MEM_HEADS * HEAD_DIM].reshape(bp, n_mem, MEM_HEADS, HEAD_DIM)
    mv_p = zmem[:, MEM_HEADS * HEAD_DIM:].reshape(bp, n_mem, MEM_HEADS, HEAD_DIM)
    om_p = _mem_attn(zp.reshape(bp, seq, D_Z), mk_p, mv_p, seqs_per_step=1, rows_per_step=min(512, seq))
    y_prompt = _merge(xp2, omla_p, od_p, om_p.reshape(bp * seq, -1), zp, wout, fg).reshape(bp, seq, d)

    xs2 = x_sample.reshape(nb * t_new, d)
    tm_s = min(256, nb * t_new)
    pos_s = past + (jnp.arange(tm_s, dtype=I32) % t_new)
    cos_s, sin_s = _rope_tables(pos_s)
    zs, (ckv_s, krope_s, ki_s, kd_s, vd_s, _, _, _, _, _, qcat_s) = branch(xs2, cos_s, sin_s, F32)
    wi_s = zs[:, C_WI:C_WI + IDX_HEADS].reshape(nb, t_new, IDX_HEADS // 2, 2)
    rows_i = (IDX_HEADS // 2) * t_new
    w_even = jnp.transpose(wi_s[..., 0], (0, 2, 1)).reshape(nb, rows_i, 1)
    w_odd = jnp.transpose(wi_s[..., 1], (0, 2, 1)).reshape(nb, rows_i, 1)
    qi_eo = jnp.transpose(zs[:, C_QI:C_QI + IDX_HEADS * IDX_DIM].reshape(nb, t_new, IDX_HEADS // 2, 2, IDX_DIM),
                          (0, 3, 2, 1, 4)).reshape(nb, 2, rows_i, IDX_DIM)
    krt_pool = jnp.swapaxes(cache_mla_krope, 2, 3)
    kit_pool = jnp.swapaxes(cache_idx_k, 2, 3)
    scores, omla_s = _sample_mla(page_table, qcat_s, qi_eo, w_even, w_odd, ckv_s, krope_s, ki_s, wv,
                                 cache_mla_ckv, krt_pool, kit_pool, nb=nb, t_new=t_new, c_pages=c_pages)
    thr = _sample_thr(scores, topk=topk_s)
    od_s = _sample_dsa(page_table, t5_bias, zs, scores, thr, kd_s, vd_s, cache_dsa_k, cache_dsa_v,
                       nb=nb, t_new=t_new, c_pages=c_pages)
    om_s = _mem_attn(zs.reshape(nb, t_new, D_Z), cache_mem_k[l], cache_mem_v[l],
                     seqs_per_step=min(4, nb), rows_per_step=t_new)
    y_sample = _merge(xs2, omla_s, od_s, om_s.reshape(nb * t_new, -1), zs, wout, fg).reshape(nb, t_new, d)

    st = lambda a, *shape: a.reshape((1,) + shape)
    return (y_prompt, y_sample,
            st(ckv_p, bp, seq, KV_LORA), st(krope_p, bp, seq, QK_ROPE),
            st(kd_p, bp, seq, DSA_KV_HEADS, HEAD_DIM), st(vd_p, bp, seq, DSA_KV_HEADS, HEAD_DIM),
            st(ki_p, bp, seq, IDX_DIM), st(mk_p, bp, n_mem, MEM_HEADS, HEAD_DIM), st(mv_p, bp, n_mem, MEM_HEADS, HEAD_DIM),
            st(ckv_s, nb, t_new, KV_LORA), st(krope_s, nb, t_new, QK_ROPE),
            st(kd_s, nb, t_new, DSA_KV_HEADS, HEAD_DIM), st(vd_s, nb, t_new, DSA_KV_HEADS, HEAD_DIM),
            st(ki_s, nb, t_new, IDX_DIM))
```

```python
import functools
import math

import numpy as np
import jax
import jax.numpy as jnp
from jax import lax
from jax.experimental import pallas as pl
from jax.experimental.pallas import tpu as pltpu

F32 = jnp.float32
BF16 = jnp.bfloat16
I32 = jnp.int32

HEAD_DIM = 128
MLA_HEADS = 8
DSA_HEADS = 4
DSA_KV_HEADS = 2
MEM_HEADS = 4
Q_LORA = 512
KV_LORA = 256
QK_ROPE = 64
IDX_HEADS = 16
IDX_DIM = 64
TOPK_MAX = 256
PAGE_SIZE = 128
NUM_BUCKETS = 32
ROPE_THETA = 10000.0
MLA_SCALE = (HEAD_DIM + QK_ROPE) ** -0.5
DSA_SCALE = HEAD_DIM ** -0.5
MEM_SCALE = HEAD_DIM ** -0.5
IDX_W_SCALE = (IDX_DIM ** -0.5) * (IDX_HEADS ** -0.5)
EPS = 1e-6
NEG_INF = float("-inf")
F32_MAX = float(np.finfo(np.float32).max)

C_Q, C_KV, C_KRKI, C_WI, C_GMLA, C_QD, C_KD, C_VD, C_QI, C_GDSA, C_QM, C_GMEM, D_Z = (
    0, 512, 768, 896, 1024, 2048, 2560, 2816, 3072, 4096, 4608, 5120, 5632)

T5_BUCKET_START = tuple(range(17)) + (19, 21, 24, 27, 31, 35, 40, 46, 52, 59, 67, 77, 87, 99, 113)
T5_FAR = T5_BUCKET_START[-1]

VMEM_LIMIT = 56 * 1024 * 1024


def _cparams(sem):
    return pltpu.CompilerParams(dimension_semantics=sem, vmem_limit_bytes=VMEM_LIMIT)


def _dot_nt(a, b):
    return lax.dot_general(a, b, (((1,), (1,)), ((), ())), preferred_element_type=F32)


def _dot(a, b):
    return jnp.dot(a, b, preferred_element_type=F32)


def _rms(x, g):
    return x * lax.rsqrt(jnp.mean(x * x, axis=-1, keepdims=True) + EPS) * g


def _inproj_kernel(x_ref, g_ref, w_ref, z_ref, h_ref):
    @pl.when(pl.program_id(1) == 0)
    def _():
        h_ref[...] = _rms(x_ref[...], g_ref[...]).astype(BF16)

    z_ref[...] = _dot(h_ref[...], w_ref[...])


def _in_proj(x, g, w, tn=512):
    n, d = x.shape
    m = w.shape[1]
    tm = min(1024, n)
    return pl.pallas_call(
        _inproj_kernel,
        out_shape=jax.ShapeDtypeStruct((n, m), F32),
        grid=(n // tm, m // tn),
        in_specs=[pl.BlockSpec((tm, d), lambda i, j: (i, 0)),
                  pl.BlockSpec((1, d), lambda i, j: (0, 0)),
                  pl.BlockSpec((d, tn), lambda i, j: (0, j))],
        out_specs=pl.BlockSpec((tm, tn), lambda i, j: (i, j)),
        scratch_shapes=[pltpu.VMEM((tm, d), BF16)],
        compiler_params=_cparams(("parallel", "arbitrary")),
        name="in_proj",
    )(x, g, w)


def _post_kernel(cq_ref, ckv_ref, krki_ref, kd_ref, vd_ref, cos_ref, sin_ref, qag_ref, kvag_ref, idxg_ref,
                 wuq_ref, wk_ref,
                 ckv_o, krope_o, ki_o, kd_o, vd_o, kcat_o, kip0_o, kip1_o, kdb_o, vdb_o, qcat_o):
    tm = cq_ref.shape[0]
    lane = lax.broadcasted_iota(I32, (tm, 128), 1)
    hi = lane >= 64
    second_half = (lane % 64) >= 32
    cos_t = cos_ref[...]
    sin_t = sin_ref[...]

    def rope128(x):
        swapped = jnp.where(second_half, pltpu.roll(x, 32, 1), pltpu.roll(x, 96, 1))
        return x * cos_t + swapped * sin_t

    ckv_n = _rms(ckv_ref[...], kvag_ref[...])
    ckv_o[...] = ckv_n

    krki = krki_ref[...]
    rot = rope128(krki)
    krope_o[...] = rot[:, :QK_ROPE]
    ms = jnp.sum(jnp.where(hi, krki * krki, 0.0), axis=-1, keepdims=True) * (1.0 / IDX_DIM)
    ki_hi = krki * lax.rsqrt(ms + EPS) * idxg_ref[...]
    ki_lo = pltpu.roll(ki_hi, 64, 1)
    ki_o[...] = ki_lo[:, :IDX_DIM]
    kip0_o[...] = jnp.where(hi, 0.0, ki_lo).astype(BF16)
    kip1_o[...] = jnp.where(hi, ki_hi, 0.0).astype(BF16)
    kcat_o[:, :KV_LORA] = ckv_n.astype(BF16)
    kcat_o[:, KV_LORA:] = jnp.where(hi, 0.0, rot).astype(BF16)

    kd = kd_ref[...]
    vd = vd_ref[...]
    kd_o[...] = kd
    vd_o[...] = vd
    kdb_o[...] = kd.astype(BF16)
    vdb_o[...] = vd.astype(BF16)

    cqn = _rms(cq_ref[...], qag_ref[...]).astype(BF16)
    qall = _dot(cqn, wuq_ref[...])
    nope_w = MLA_HEADS * HEAD_DIM
    for h in range(MLA_HEADS):
        qn = qall[:, h * HEAD_DIM:(h + 1) * HEAD_DIM].astype(BF16)
        qcat_o[h, :, :KV_LORA] = _dot(qn, wk_ref[h]).astype(qcat_o.dtype)
    for c in range(MLA_HEADS // 2):
        rc = rope128(qall[:, nope_w + 128 * c:nope_w + 128 * (c + 1)])
        qcat_o[2 * c, :, KV_LORA:] = jnp.where(hi, 0.0, rc).astype(qcat_o.dtype)
        qcat_o[2 * c + 1, :, KV_LORA:] = jnp.where(hi, 0.0, pltpu.roll(rc, 64, 1)).astype(qcat_o.dtype)


def _post(z, cos_t, sin_t, qag, kvag, idxg128, wuq, wk, q_dtype):
    n = z.shape[0]
    tm = min(256, n)
    nt = cos_t.shape[0] // tm
    row = lambda c, w: pl.BlockSpec((tm, w), lambda i, c=c, w=w: (i, c // w))
    full2 = lambda a: pl.BlockSpec(a.shape, lambda i: (0, 0))
    full3 = lambda a: pl.BlockSpec(a.shape, lambda i: (0, 0, 0))
    tab = pl.BlockSpec((tm, 128), lambda i: (i % nt, 0))
    out_shapes = [
        jax.ShapeDtypeStruct((n, KV_LORA), F32), jax.ShapeDtypeStruct((n, QK_ROPE), F32),
        jax.ShapeDtypeStruct((n, IDX_DIM), F32), jax.ShapeDtypeStruct((n, 256), F32),
        jax.ShapeDtypeStruct((n, 256), F32), jax.ShapeDtypeStruct((n, 384), BF16),
        jax.ShapeDtypeStruct((n, 128), BF16), jax.ShapeDtypeStruct((n, 128), BF16),
        jax.ShapeDtypeStruct((n, 256), BF16), jax.ShapeDtypeStruct((n, 256), BF16),
        jax.ShapeDtypeStruct((MLA_HEADS, n, 384), q_dtype)]
    o2 = lambda w: pl.BlockSpec((tm, w), lambda i: (i, 0))
    out_specs = [o2(KV_LORA), o2(QK_ROPE), o2(IDX_DIM), o2(256), o2(256), o2(384), o2(128), o2(128), o2(256),
                 o2(256), pl.BlockSpec((MLA_HEADS, tm, 384), lambda i: (0, i, 0))]
    return pl.pallas_call(
        _post_kernel,
        out_shape=out_shapes,
        grid=(n // tm,),
        in_specs=[row(C_Q, 512), row(C_KV, 256), row(C_KRKI, 128), row(C_KD, 256), row(C_VD, 256), tab, tab,
                  full2(qag), full2(kvag), full2(idxg128), full2(wuq), full3(wk)],
        out_specs=out_specs,
        compiler_params=_cparams(("parallel",)),
        name="post",
    )(z, z, z, z, z, cos_t, sin_t, qag, kvag, idxg128, wuq, wk)


def _top16(x):
    return lax.bitcast_convert_type(lax.bitcast_convert_type(x, I32) & np.int32(-65536), F32)


def _count_lanes(acc):
    return jnp.sum(acc.astype(F32), axis=-1, keepdims=True).astype(I32)


def _topk_threshold(count_hi, count_f32, k, rows):
    sign = np.int32(-2 ** 31)
    key_neg_inf = np.int32(-2139095041)

    def to_float(u):
        key = u ^ sign
        bits = jnp.where(key >= 0, key, key ^ np.int32(0x7FFFFFFF))
        return key, lax.bitcast_convert_type(bits, F32)

    def step(b, prefix, done, coarse):
        cand = prefix | jnp.left_shift(np.int32(1), jnp.asarray(31 - b, I32))
        key, cand_f = to_float(cand)
        n = count_hi(_top16(cand_f)) if coarse else count_f32(cand_f)
        ok = (n >= k) | (key <= key_neg_inf)
        return jnp.where(ok, cand, prefix), done | jnp.where(n == k, 1, 0)

    zero = jnp.zeros((rows, 1), I32)
    prefix, done = lax.fori_loop(0, 16, lambda b, c: step(b, c[0], c[1], True), (zero, zero))

    def fine(c):
        b, prefix, done, _ = c
        prefix, done = step(b, prefix, done, False)
        return b + 1, prefix, done, jnp.min(done)

    _, prefix, done, _ = lax.while_loop(lambda c: (c[0] < 32) & (c[3] == 0), fine,
                                        (jnp.int32(16), prefix, done, jnp.min(done)))
    key, thr = to_float(prefix)
    return jnp.where(key < key_neg_inf, NEG_INF, thr), done


def _tie_cutoff(count_eq_before, need, nbits, rows):
    def step(b, p):
        cand = p | jnp.left_shift(np.int32(1), jnp.asarray(nbits - 1 - b, I32))
        return jnp.where(count_eq_before(cand) < need, cand, p)
    return lax.fori_loop(0, nbits, step, jnp.zeros((rows, 1), I32))


def _t5_bias_tile(dist, t5_ref, heads):
    ge = [dist >= T5_BUCKET_START[b] for b in range(1, NUM_BUCKETS)]
    out = []
    for h in heads:
        bias = jnp.full(dist.shape, t5_ref[0, h], F32)
        for b in range(1, NUM_BUCKETS):
            bias = jnp.where(ge[b - 1], t5_ref[b, h], bias)
        out.append(bias)
    return out


def _softmax_update(s, m, l):
    m_new = jnp.maximum(m, jnp.max(s, axis=-1, keepdims=True))
    m_use = jnp.where(m_new == NEG_INF, 0.0, m_new)
    alpha = jnp.exp(m - m_use)
    p = jnp.exp(s - m_use)
    return m_new, alpha * l + jnp.sum(p, axis=-1, keepdims=True), alpha, p


def _prompt_attn_kernel(t5_ref, qcat_ref, qd_ref, qi_ref, wi_ref, kcat_ref, kip0_ref, kip1_ref, kd_ref, vd_ref,
                        wv_ref, omla_ref, od_ref, sc_ref, hi_ref, acc_ref, accd_ref, band_ref, *, t, topk):
    i = pl.program_id(1)
    rows_m = MLA_HEADS * t
    row = lax.broadcasted_iota(I32, (t, t), 0)
    col = lax.broadcasted_iota(I32, (t, t), 1)
    vis = col <= row

    @pl.when((pl.program_id(0) == 0) & (i == 0))
    def _():
        left = _t5_bias_tile(row - col + t, t5_ref, range(DSA_HEADS))
        diag = _t5_bias_tile(row - col, t5_ref, range(DSA_HEADS))
        for h in range(DSA_HEADS):
            band_ref[h, :, :t] = left[h]
            band_ref[h, :, t:] = diag[h]

    qall = qcat_ref[...].reshape(rows_m, 384)
    qi_b = qi_ref[...].astype(BF16)
    wsc = wi_ref[...] * IDX_W_SCALE
    acc_ref[...] = jnp.zeros_like(acc_ref)

    def kv_step(j, carry, masked):
        m, l = carry
        start = pl.multiple_of(j * t, t)
        kc = kcat_ref[pl.ds(start, t), :]
        s = _dot_nt(qall, kc) * MLA_SCALE
        if masked:
            s = jnp.where(vis[None], s.reshape(MLA_HEADS, t, t), NEG_INF).reshape(rows_m, t)
        m_new, l_new, alpha, p = _softmax_update(s, m, l)
        acc_ref[...] = acc_ref[...] * alpha + _dot(p.astype(BF16), kc[:, :KV_LORA])
        kp = (kip0_ref[pl.ds(start, t), :], kip1_ref[pl.ds(start, t), :])
        sc = jnp.zeros((t, t), F32)
        for h in range(IDX_HEADS):
            d = _dot_nt(qi_b[:, 128 * (h // 2):128 * (h // 2 + 1)], kp[h % 2])
            sc = sc + jnp.maximum(d, 0.0) * wsc[:, h:h + 1]
        if masked:
            sc = jnp.where(vis, sc, NEG_INF)
        sc_ref[j] = sc
        hi_ref[j] = _top16(sc).astype(BF16)
        return m_new, l_new

    init = (jnp.full((rows_m, 1), NEG_INF, F32), jnp.zeros((rows_m, 1), F32))
    carry = lax.fori_loop(0, i, functools.partial(kv_step, masked=False), init)
    m_mla, l_mla = kv_step(i, carry, masked=True)

    lanes = t // 128
    col_t = lax.broadcasted_iota(I32, (t, t), 1)

    def count_tiles(pred, dtype):
        def body(j, acc):
            hit = jnp.where(pred(j), jnp.ones((), dtype), jnp.zeros((), dtype))
            for u in range(lanes):
                acc = acc + hit[:, u * 128:(u + 1) * 128]
            return acc
        return _count_lanes(lax.fori_loop(0, i + 1, body, jnp.zeros((t, 128), dtype)))

    def count_hi(c):
        cb = c.astype(BF16)
        return count_tiles(lambda j: hi_ref[j] >= cb, BF16)

    thr, done = _topk_threshold(count_hi, lambda c: count_tiles(lambda j: sc_ref[j] >= c, I32), topk, t)
    thr = jnp.maximum(thr, -F32_MAX)

    @pl.when(jnp.min(done) == 0)
    def _():
        need = topk - count_tiles(lambda j: sc_ref[j] > thr, I32)
        cut = _tie_cutoff(lambda p: count_tiles(lambda j: (sc_ref[j] == thr) & (j * t + col_t < p), I32),
                          need, max(1, (sc_ref.shape[0] * t - 1).bit_length()), t)

        def drop(j, _):
            x = sc_ref[j]
            sc_ref[j] = jnp.where((x == thr) & (j * t + col_t > cut), NEG_INF, x)
            return 0
        lax.fori_loop(0, i + 1, drop, 0)

    qd_b = qd_ref[...].astype(BF16)
    qg = [jnp.concatenate([qd_b[:, (2 * g) * 128:(2 * g + 1) * 128],
                           qd_b[:, (2 * g + 1) * 128:(2 * g + 2) * 128]], axis=0) for g in range(DSA_KV_HEADS)]
    accd_ref[...] = jnp.zeros_like(accd_ref)

    def dsa_step(j, carry, kind):
        start = pl.multiple_of(j * t, t)
        sel = sc_ref[j] >= thr
        new = []
        for g in range(DSA_KV_HEADS):
            m, l = carry[g]
            kd = kd_ref[pl.ds(start, t), g * 128:(g + 1) * 128]
            vd = vd_ref[pl.ds(start, t), g * 128:(g + 1) * 128]
            s = _dot_nt(qg[g], kd) * DSA_SCALE
            parts = []
            for r in range(2):
                h = 2 * g + r
                if kind == "far":
                    b = t5_ref[NUM_BUCKETS - 1, h]
                elif kind == "left":
                    b = band_ref[h, :, :t]
                else:
                    b = band_ref[h, :, t:]
                parts.append(jnp.where(sel, s[r * t:(r + 1) * t] + b, NEG_INF))
            s = jnp.concatenate(parts, axis=0)
            m_new, l_new, alpha, p = _softmax_update(s, m, l)
            accd_ref[g] = accd_ref[g] * alpha + _dot(p.astype(BF16), vd)
            new.append((m_new, l_new))
        return tuple(new)

    init_d = tuple((jnp.full((2 * t, 1), NEG_INF, F32), jnp.zeros((2 * t, 1), F32)) for _ in range(DSA_KV_HEADS))
    carry_d = lax.fori_loop(0, jnp.maximum(i - 1, 0), functools.partial(dsa_step, kind="far"), init_d)
    carry_d = lax.fori_loop(jnp.maximum(i - 1, 0), i, functools.partial(dsa_step, kind="left"), carry_d)
    carry_d = dsa_step(i, carry_d, "diag")

    for g in range(DSA_KV_HEADS):
        o = accd_ref[g] / carry_d[g][1]
        for r in range(2):
            h = 2 * g + r
            od_ref[:, h * 128:(h + 1) * 128] = o[r * t:(r + 1) * t]
    lat = (acc_ref[...] / l_mla).astype(BF16)
    for h in range(MLA_HEADS):
        omla_ref[:, h * 128:(h + 1) * 128] = _dot(lat[h * t:(h + 1) * t], wv_ref[h])


def _prompt_attn(t5, qcat8, z, kcat, kip0, kip1, kdb, vdb, wv, *, batch, seq, t, topk):
    assert t >= T5_FAR and seq % t == 0 and seq // 128 <= 256
    n = batch * seq
    nq = seq // t
    qrow = lambda c, w: pl.BlockSpec((t, w), lambda b, i, c=c, w=w: (b * nq + i, c // w))
    kfull = lambda w: pl.BlockSpec((seq, w), lambda b, i: (b, 0))
    kern = functools.partial(_prompt_attn_kernel, t=t, topk=topk)
    return pl.pallas_call(
        kern,
        out_shape=[jax.ShapeDtypeStruct((n, MLA_HEADS * HEAD_DIM), F32),
                   jax.ShapeDtypeStruct((n, DSA_HEADS * HEAD_DIM), F32)],
        grid=(batch, nq),
        in_specs=[pl.BlockSpec(memory_space=pltpu.SMEM),
                  pl.BlockSpec((MLA_HEADS, t, 384), lambda b, i: (0, b * nq + i, 0)),
                  qrow(C_QD, 512), qrow(C_QI, 1024), qrow(C_WI, 128),
                  kfull(384), kfull(128), kfull(128), kfull(256), kfull(256),
                  pl.BlockSpec(wv.shape, lambda b, i: (0, 0, 0))],
        out_specs=[pl.BlockSpec((t, MLA_HEADS * HEAD_DIM), lambda b, i: (b * nq + i, 0)),
                   pl.BlockSpec((t, DSA_HEADS * HEAD_DIM), lambda b, i: (b * nq + i, 0))],
        scratch_shapes=[pltpu.VMEM((nq, t, t), F32),
                        pltpu.VMEM((nq, t, t), BF16),
                        pltpu.VMEM((MLA_HEADS * t, KV_LORA), F32),
                        pltpu.VMEM((DSA_KV_HEADS, 2 * t, HEAD_DIM), F32),
                        pltpu.VMEM((DSA_HEADS, t, 2 * t), F32)],
        compiler_params=_cparams(("arbitrary", "arbitrary")),
        name="prompt_attn",
    )(t5, qcat8, z, z, z, kcat, kip0, kip1, kdb, vdb, wv)


def _pad_rows(x, rows):
    return jnp.concatenate([x, jnp.zeros((rows - x.shape[0], x.shape[1]), x.dtype)], axis=0)


def _next_chunk(n, c, nch):
    last = c + 1 == nch
    return jnp.where(last, n + 1, n), jnp.where(last, 0, c + 1)


def _sample_mla_kernel(pt_ref, qcat_ref, qi_ref, we_ref, wo_ref, ckvn_ref, krn_ref, kin_ref, wv_ref,
                       ckv_hbm, krt_hbm, kit_hbm, sc_ref, omla_ref, ckv_buf, krt_buf, kit_buf, sem, *, c_pages):
    n = pl.program_id(0)
    nb = pl.num_programs(0)
    nch = pt_ref.shape[1] // c_pages
    cw = c_pages * PAGE_SIZE
    past = nch * cw
    t_new = ckvn_ref.shape[0]
    rows_m = MLA_HEADS * t_new

    def chunk_copies(seq, c, slot):
        cps = []
        for u in range(c_pages):
            page = pt_ref[seq, c * c_pages + u]
            rows = pl.ds(u * PAGE_SIZE, PAGE_SIZE)
            cps.append(pltpu.make_async_copy(ckv_hbm.at[0, page], ckv_buf.at[slot, rows, :], sem.at[0, slot]))
            cps.append(pltpu.make_async_copy(krt_hbm.at[0, page], krt_buf.at[slot, :, rows], sem.at[1, slot]))
            cps.append(pltpu.make_async_copy(kit_hbm.at[0, page], kit_buf.at[slot, :, rows], sem.at[2, slot]))
        return cps

    @pl.when(n == 0)
    def _():
        for cp in chunk_copies(0, 0, 0):
            cp.start()

    q = qcat_ref[...].reshape(rows_m, 384).astype(BF16)
    ql = q[:, :KV_LORA]
    qr = q[:, KV_LORA:KV_LORA + QK_ROPE]
    rows_i = qi_ref.shape[1]
    q_idx = qi_ref[...].reshape(2 * rows_i, IDX_DIM).astype(BF16)
    w_idx = jnp.concatenate([we_ref[...], wo_ref[...]], axis=0) * IDX_W_SCALE

    def idx_scores(dots):
        d = jnp.maximum(dots, 0.0) * w_idx
        return jnp.sum(d.reshape(IDX_HEADS, t_new, d.shape[-1]), axis=0)

    def chunk(c, carry):
        m, l, acc = carry
        slot = c % 2
        seq2, c2 = _next_chunk(n, c, nch)
        for cp in chunk_copies(jnp.minimum(seq2, nb - 1), c2, 1 - slot):
            cp.start()
        for cp in chunk_copies(n, c, slot):
            cp.wait()
        ck = ckv_buf[slot].astype(BF16)
        s = (_dot_nt(ql, ck) + _dot(qr, krt_buf[slot].astype(BF16))) * MLA_SCALE
        m, l, alpha, p = _softmax_update(s, m, l)
        acc = acc * alpha + _dot(p.astype(BF16), ck)
        kit = kit_buf[slot].astype(BF16)
        sc_ref[:, pl.ds(pl.multiple_of(c * cw, cw), cw)] = idx_scores(_dot(q_idx, kit))
        return m, l, acc

    init = (jnp.full((rows_m, 1), NEG_INF, F32), jnp.zeros((rows_m, 1), F32), jnp.zeros((rows_m, KV_LORA), F32))
    m, l, acc = lax.fori_loop(0, nch, chunk, init)

    @pl.when(n == nb - 1)
    def _():
        for cp in chunk_copies(nb - 1, 0, 0):
            cp.wait()

    tok = lax.broadcasted_iota(I32, (t_new, PAGE_SIZE), 0)
    col = lax.broadcasted_iota(I32, (t_new, PAGE_SIZE), 1)
    vis = col <= tok
    ck = _pad_rows(ckvn_ref[...], PAGE_SIZE).astype(BF16)
    kr = _pad_rows(krn_ref[...], PAGE_SIZE).astype(BF16)
    ki = _pad_rows(kin_ref[...], PAGE_SIZE).astype(BF16)
    s = (_dot_nt(ql, ck) + _dot_nt(qr, kr)) * MLA_SCALE
    s = jnp.where(vis[None], s.reshape(MLA_HEADS, t_new, PAGE_SIZE), NEG_INF).reshape(rows_m, PAGE_SIZE)
    m, l, alpha, p = _softmax_update(s, m, l)
    acc = acc * alpha + _dot(p.astype(BF16), ck)
    sc_ref[:, past:] = jnp.where(vis, idx_scores(_dot_nt(q_idx, ki)), NEG_INF)
    lat = (acc / l).astype(BF16)
    for h in range(MLA_HEADS):
        omla_ref[:, h * 128:(h + 1) * 128] = _dot(lat[h * t_new:(h + 1) * t_new], wv_ref[h])


def _sample_mla(page_table, qcat8, qi_eo, w_even, w_odd, ckv_new, kr_new, ki_new, wv, ckv_pool, krt_pool, kit_pool,
                *, nb, t_new, c_pages):
    npages = page_table.shape[1]
    nch = npages // c_pages
    assert nch * c_pages == npages and nch % 2 == 0
    cw = c_pages * PAGE_SIZE
    width = npages * PAGE_SIZE + PAGE_SIZE
    rows_i = (IDX_HEADS // 2) * t_new
    any_spec = pl.BlockSpec(memory_space=pl.ANY)
    in_specs = [pl.BlockSpec((MLA_HEADS, t_new, 384), lambda n, pt: (0, n, 0)),
                pl.BlockSpec((None, 2, rows_i, IDX_DIM), lambda n, pt: (n, 0, 0, 0)),
                pl.BlockSpec((None, rows_i, 1), lambda n, pt: (n, 0, 0)),
                pl.BlockSpec((None, rows_i, 1), lambda n, pt: (n, 0, 0)),
                pl.BlockSpec((t_new, KV_LORA), lambda n, pt: (n, 0)),
                pl.BlockSpec((t_new, QK_ROPE), lambda n, pt: (n, 0)),
                pl.BlockSpec((t_new, IDX_DIM), lambda n, pt: (n, 0)),
                pl.BlockSpec(wv.shape, lambda n, pt: (0, 0, 0)),
                any_spec, any_spec, any_spec]
    grid_spec = pltpu.PrefetchScalarGridSpec(
        num_scalar_prefetch=1, grid=(nb,), in_specs=in_specs,
        out_specs=[pl.BlockSpec((None, t_new, width), lambda n, pt: (n, 0, 0)),
                   pl.BlockSpec((t_new, MLA_HEADS * HEAD_DIM), lambda n, pt: (n, 0))],
        scratch_shapes=[pltpu.VMEM((2, cw, KV_LORA), F32), pltpu.VMEM((2, QK_ROPE, cw), F32),
                        pltpu.VMEM((2, IDX_DIM, cw), F32), pltpu.SemaphoreType.DMA((3, 2))])
    return pl.pallas_call(
        functools.partial(_sample_mla_kernel, c_pages=c_pages),
        out_shape=[jax.ShapeDtypeStruct((nb, t_new, width), F32),
                   jax.ShapeDtypeStruct((nb * t_new, MLA_HEADS * HEAD_DIM), F32)],
        grid_spec=grid_spec,
        compiler_params=_cparams(("arbitrary",)),
        name="sample_mla",
    )(page_table, qcat8, qi_eo, w_even, w_odd, ckv_new, kr_new, ki_new, wv, ckv_pool, krt_pool, kit_pool)


def _sample_thr_kernel(sc_ref, thr_ref, cut_ref, hi_ref, *, topk, chunk):
    nseq, t_new, width = sc_ref.shape
    rows = nseq * t_new
    lanes = chunk // 128
    col_c = lax.broadcasted_iota(I32, (rows, chunk), 1)

    def load(c_idx):
        return sc_ref[:, :, pl.ds(pl.multiple_of(c_idx * chunk, chunk), chunk)].reshape(rows, chunk)

    def fill(c_idx, _):
        hi_ref[:, pl.ds(pl.multiple_of(c_idx * chunk, chunk), chunk)] = _top16(load(c_idx)).astype(BF16)
        return 0
    lax.fori_loop(0, width // chunk, fill, 0)

    def count_chunks(pred, dtype):
        def body(c_idx, acc):
            hit = jnp.where(pred(c_idx), jnp.ones((), dtype), jnp.zeros((), dtype))
            for u in range(lanes):
                acc = acc + hit[:, u * 128:(u + 1) * 128]
            return acc
        return _count_lanes(lax.fori_loop(0, width // chunk, body, jnp.zeros((rows, 128), dtype)))

    def count_hi(c):
        cb = c.astype(BF16)
        return count_chunks(lambda ci: hi_ref[:, pl.ds(pl.multiple_of(ci * chunk, chunk), chunk)] >= cb, BF16)

    thr, done = _topk_threshold(count_hi, lambda c: count_chunks(lambda ci: load(ci) >= c, I32), topk, rows)
    thr = jnp.maximum(thr, -F32_MAX)
    thr_ref[...] = thr
    cut_ref[...] = jnp.full((rows, 1), np.iinfo(np.int32).max, I32)

    @pl.when(jnp.min(done) == 0)
    def _():
        need = topk - count_chunks(lambda ci: load(ci) > thr, I32)
        cut_ref[...] = _tie_cutoff(
            lambda p: count_chunks(lambda ci: (load(ci) == thr) & (ci * chunk + col_c < p), I32),
            need, max(1, (width - 1).bit_length()), rows)


def _sample_thr(scores, *, topk, nseq=8):
    nb, t_new, width = scores.shape
    nseq = min(nseq, nb)
    lanes = width // 128
    assert lanes <= 256
    chunk = 128 * max(d for d in range(1, 5) if lanes % d == 0)
    rows = nseq * t_new
    return pl.pallas_call(
        functools.partial(_sample_thr_kernel, topk=topk, chunk=chunk),
        out_shape=[jax.ShapeDtypeStruct((nb * t_new, 1), F32), jax.ShapeDtypeStruct((nb * t_new, 1), I32)],
        grid=(nb // nseq,),
        in_specs=[pl.BlockSpec((nseq, t_new, width), lambda i: (i, 0, 0))],
        out_specs=[pl.BlockSpec((rows, 1), lambda i: (i, 0)), pl.BlockSpec((rows, 1), lambda i: (i, 0))],
        scratch_shapes=[pltpu.VMEM((rows, width), BF16)],
        compiler_params=_cparams(("parallel",)),
        name="sample_thr",
    )(scores)


def _sample_dsa_kernel(pt_ref, t5_ref, qd_ref, sc_ref, thr_ref, cut_ref, kdn_ref, vdn_ref, kd_hbm, vd_hbm, od_ref,
                       kd_buf, vd_buf, sem, *, c_pages):
    n = pl.program_id(0)
    nb = pl.num_programs(0)
    nch = pt_ref.shape[1] // c_pages
    cw = c_pages * PAGE_SIZE
    past = nch * cw
    t_new = qd_ref.shape[0]

    def chunk_copies(seq, c, slot):
        cps = []
        for u in range(c_pages):
            page = pt_ref[seq, c * c_pages + u]
            rows = pl.ds(u * PAGE_SIZE, PAGE_SIZE)
            for g in range(DSA_KV_HEADS):
                cps.append(pltpu.make_async_copy(kd_hbm.at[0, page, :, g, :], kd_buf.at[slot, g, rows, :], sem.at[0, slot]))
                cps.append(pltpu.make_async_copy(vd_hbm.at[0, page, :, g, :], vd_buf.at[slot, g, rows, :], sem.at[1, slot]))
        return cps

    @pl.when(n == 0)
    def _():
        for cp in chunk_copies(0, 0, 0):
            cp.start()

    qd_b = qd_ref[...].astype(BF16)
    qg = [jnp.concatenate([qd_b[:, (2 * g) * 128:(2 * g + 1) * 128],
                           qd_b[:, (2 * g + 1) * 128:(2 * g + 2) * 128]], axis=0) for g in range(DSA_KV_HEADS)]
    thr = thr_ref[...]
    cut = cut_ref[...]

    def selected(start, width):
        x = sc_ref[:, pl.ds(start, width)]
        col = start + lax.broadcasted_iota(I32, (t_new, width), 1)
        return (x > thr) | ((x == thr) & (col <= cut))

    def attend(carry, k_of, v_of, sel, dist):
        if dist is not None:
            bias = _t5_bias_tile(dist, t5_ref, range(DSA_HEADS))
        new = []
        for g in range(DSA_KV_HEADS):
            m, l, acc = carry[g]
            s = _dot_nt(qg[g], k_of(g)) * DSA_SCALE
            parts = []
            for r in range(2):
                h = 2 * g + r
                b = bias[h] if dist is not None else t5_ref[NUM_BUCKETS - 1, h]
                parts.append(jnp.where(sel, s[r * t_new:(r + 1) * t_new] + b, NEG_INF))
            m_new, l_new, alpha, p = _softmax_update(jnp.concatenate(parts, axis=0), m, l)
            new.append((m_new, l_new, acc * alpha + _dot(p.astype(BF16), v_of(g))))
        return tuple(new)

    def chunk(c, carry, near):
        slot = c % 2
        seq2, c2 = _next_chunk(n, c, nch)
        for cp in chunk_copies(jnp.minimum(seq2, nb - 1), c2, 1 - slot):
            cp.start()
        for cp in chunk_copies(n, c, slot):
            cp.wait()
        start = pl.multiple_of(c * cw, cw)
        dist = None
        if near:
            tok = lax.broadcasted_iota(I32, (t_new, cw), 0)
            col = lax.broadcasted_iota(I32, (t_new, cw), 1)
            dist = (past + tok) - (start + col)
        return attend(carry, lambda g: kd_buf[slot, g].astype(BF16), lambda g: vd_buf[slot, g].astype(BF16),
                      selected(start, cw), dist)

    init = tuple((jnp.full((2 * t_new, 1), NEG_INF, F32), jnp.zeros((2 * t_new, 1), F32),
                  jnp.zeros((2 * t_new, HEAD_DIM), F32)) for _ in range(DSA_KV_HEADS))
    carry = lax.fori_loop(0, nch - 1, functools.partial(chunk, near=False), init)
    carry = chunk(nch - 1, carry, near=True)

    @pl.when(n == nb - 1)
    def _():
        for cp in chunk_copies(nb - 1, 0, 0):
            cp.wait()

    tok = lax.broadcasted_iota(I32, (t_new, PAGE_SIZE), 0)
    col = lax.broadcasted_iota(I32, (t_new, PAGE_SIZE), 1)
    kdn = _pad_rows(kdn_ref[...], PAGE_SIZE).astype(BF16)
    vdn = _pad_rows(vdn_ref[...], PAGE_SIZE).astype(BF16)
    carry = attend(carry, lambda g: kdn[:, g * 128:(g + 1) * 128], lambda g: vdn[:, g * 128:(g + 1) * 128],
                   selected(past, PAGE_SIZE), tok - col)
    for g in range(DSA_KV_HEADS):
        o = carry[g][2] / carry[g][1]
        for r in range(2):
            h = 2 * g + r
            od_ref[:, h * 128:(h + 1) * 128] = o[r * t_new:(r + 1) * t_new]


def _sample_dsa(page_table, t5, z, scores, thr, cut, kd_new, vd_new, kd_pool, vd_pool, *, nb, t_new, c_pages):
    npages = page_table.shape[1]
    nch = npages // c_pages
    assert nch * c_pages == npages and nch % 2 == 0 and c_pages * PAGE_SIZE >= T5_FAR
    cw = c_pages * PAGE_SIZE
    width = scores.shape[-1]
    any_spec = pl.BlockSpec(memory_space=pl.ANY)
    in_specs = [pl.BlockSpec(memory_space=pltpu.SMEM),
                pl.BlockSpec((t_new, 512), lambda n, pt: (n, C_QD // 512)),
                pl.BlockSpec((None, t_new, width), lambda n, pt: (n, 0, 0)),
                pl.BlockSpec((t_new, 1), lambda n, pt: (n, 0)),
                pl.BlockSpec((t_new, 1), lambda n, pt: (n, 0)),
                pl.BlockSpec((t_new, 256), lambda n, pt: (n, 0)),
                pl.BlockSpec((t_new, 256), lambda n, pt: (n, 0)),
                any_spec, any_spec]
    grid_spec = pltpu.PrefetchScalarGridSpec(
        num_scalar_prefetch=1, grid=(nb,), in_specs=in_specs,
        out_specs=pl.BlockSpec((t_new, DSA_HEADS * HEAD_DIM), lambda n, pt: (n, 0)),
        scratch_shapes=[pltpu.VMEM((2, DSA_KV_HEADS, cw, HEAD_DIM), F32), pltpu.VMEM((2, DSA_KV_HEADS, cw, HEAD_DIM), F32),
                        pltpu.SemaphoreType.DMA((2, 2))])
    return pl.pallas_call(
        functools.partial(_sample_dsa_kernel, c_pages=c_pages),
        out_shape=jax.ShapeDtypeStruct((nb * t_new, DSA_HEADS * HEAD_DIM), F32),
        grid_spec=grid_spec,
        compiler_params=_cparams(("arbitrary",)),
        name="sample_dsa",
    )(page_table, t5, z, scores, thr, cut, kd_new, vd_new, kd_pool, vd_pool)


def _mem_attn_kernel(q_ref, k_ref, v_ref, o_ref):
    for s in range(q_ref.shape[0]):
        q = q_ref[s].astype(BF16)
        for h in range(MEM_HEADS):
            k = k_ref[s, :, h, :].astype(BF16)
            v = v_ref[s, :, h, :].astype(BF16)
            logits = _dot_nt(q[:, h * 128:(h + 1) * 128], k) * MEM_SCALE
            p = jnp.exp(logits - jnp.max(logits, axis=-1, keepdims=True))
            o = _dot(p.astype(BF16), v) / jnp.sum(p, axis=-1, keepdims=True)
            o_ref[s, :, h * 128:(h + 1) * 128] = o


def _mem_attn(z3, mem_k, mem_v, *, seqs_per_step, rows_per_step):
    groups, rows, _ = z3.shape
    n_mem = mem_k.shape[1]
    gs, rs = seqs_per_step, rows_per_step
    kv_spec = pl.BlockSpec((gs, n_mem, MEM_HEADS, HEAD_DIM), lambda a, b: (a, 0, 0, 0))
    return pl.pallas_call(
        _mem_attn_kernel,
        out_shape=jax.ShapeDtypeStruct((groups, rows, MEM_HEADS * HEAD_DIM), F32),
        grid=(groups // gs, rows // rs),
        in_specs=[pl.BlockSpec((gs, rs, 512), lambda a, b: (a, b, C_QM // 512)), kv_spec, kv_spec],
        out_specs=pl.BlockSpec((gs, rs, 512), lambda a, b: (a, b, 0)),
        compiler_params=_cparams(("parallel", "arbitrary")),
        name="mem_attn",
    )(z3, mem_k, mem_v)


def _silu(g):
    return g * (1.0 / (1.0 + jnp.exp(-g)))


def _merge_kernel(x_ref, omla_ref, od_ref, om_ref, gmla_ref, gdsa_ref, gmem_ref, wout_ref, fg_ref, y_ref):
    o = jnp.concatenate([(omla_ref[...] * _silu(gmla_ref[...])).astype(BF16),
                         (od_ref[...] * _silu(gdsa_ref[...])).astype(BF16),
                         (om_ref[...] * _silu(gmem_ref[...])).astype(BF16)], axis=-1)
    xo = x_ref[...] + _dot(o, wout_ref[...])
    y_ref[...] = _rms(xo, fg_ref[...])


def _merge(x, omla, od, om, z, wout, fg):
    n, d = x.shape
    tm = min(256, n)
    row = lambda w, c=0: pl.BlockSpec((tm, w), lambda i, c=c, w=w: (i, c // w))
    return pl.pallas_call(
        _merge_kernel,
        out_shape=jax.ShapeDtypeStruct((n, d), F32),
        grid=(n // tm,),
        in_specs=[row(d), row(1024), row(512), row(512), row(1024, C_GMLA), row(512, C_GDSA), row(512, C_GMEM),
                  pl.BlockSpec(wout.shape, lambda i: (0, 0)), pl.BlockSpec((1, d), lambda i: (0, 0))],
        out_specs=row(d),
        compiler_params=_cparams(("parallel",)),
        name="merge",
    )(x, omla, od, om, z, z, z, wout, fg)


def _rope_tables(pos):
    half = QK_ROPE // 2
    inv_freq = jnp.exp(-math.log(ROPE_THETA) * jnp.arange(half, dtype=F32) / half)
    ang = pos.astype(F32)[:, None] * inv_freq[None, :]
    cos, sin = jnp.cos(ang), jnp.sin(ang)
    return jnp.tile(cos, (1, 4)), jnp.concatenate([-sin, sin, -sin, sin], axis=1)


def _permute_w_in(w):
    d = w.shape[0]
    seg = lambda a, b: w[:, a:b]
    return jnp.concatenate([
        seg(0, 512), seg(512, 768), seg(768, 832), seg(3904, 3968), seg(3968, 3984), jnp.zeros((d, 112), w.dtype),
        seg(832, 1856), seg(1856, 2368), seg(2368, 2624), seg(2624, 2880), seg(2880, 3904), seg(3984, 4496),
        seg(4496, 5008), seg(5008, 5520)], axis=1).astype(BF16)


def kernel(x_prompt, x_sample, mem_prompt, cache_mla_ckv, cache_mla_krope, cache_dsa_k, cache_dsa_v, cache_idx_k,
           cache_mem_k, cache_mem_v, page_table, t5_bias, norm_g, w_in, qa_norm_g, w_uq, kva_norm_g, w_ukv,
           idx_norm_g, mem_norm_g, w_mem_kv, w_out, final_norm_g, *, t_blk=256, c_pages=16):
    bp, seq, d = x_prompt.shape
    nb, t_new, _ = x_sample.shape
    n_mem = mem_prompt.shape[1]
    npages = page_table.shape[1]
    past = npages * PAGE_SIZE
    depth = norm_g.shape[0]
    assert depth == 1
    l = 0
    t_blk = min(t_blk, seq)
    c_pages = min(c_pages, npages // 2)
    topk_p = min(TOPK_MAX, seq // 4)
    topk_s = min(TOPK_MAX, (past + t_new) // 4)

    w_in_p = _permute_w_in(w_in[l])
    wuq = jnp.concatenate([w_uq[l][:, :, :HEAD_DIM].reshape(Q_LORA, -1),
                           w_uq[l][:, :, HEAD_DIM:].reshape(Q_LORA, -1)], axis=1).astype(BF16)
    wk = jnp.transpose(w_ukv[l][:, :, :HEAD_DIM], (1, 2, 0)).astype(BF16)
    wv = jnp.transpose(w_ukv[l][:, :, HEAD_DIM:], (1, 0, 2)).astype(BF16)
    wout = w_out[l].astype(BF16)
    wmem = w_mem_kv[l].astype(BF16)
    g_row = lambda g: g.reshape(1, -1)
    idxg128 = jnp.concatenate([jnp.zeros((1, 64), F32), g_row(idx_norm_g[l])], axis=1)
    fg = g_row(final_norm_g)

    def branch(x2, cos_t, sin_t, q_dtype):
        z = _in_proj(x2, g_row(norm_g[l]), w_in_p)
        outs = _post(z, cos_t, sin_t, g_row(qa_norm_g[l]), g_row(kva_norm_g[l]), idxg128, wuq, wk, q_dtype)
        return z, outs

    xp2 = x_prompt.reshape(bp * seq, d)
    cos_p, sin_p = _rope_tables(jnp.arange(seq, dtype=I32))
    zp, (ckv_p, krope_p, ki_p, kd_p, vd_p, kcat_p, kip0_p, kip1_p, kdb_p, vdb_p, qcat_p) = branch(xp2, cos_p, sin_p, BF16)
    omla_p, od_p = _prompt_attn(t5_bias, qcat_p, zp, kcat_p, kip0_p, kip1_p, kdb_p, vdb_p, wv,
                                batch=bp, seq=seq, t=t_blk, topk=topk_p)
    zmem = _in_proj(mem_prompt.reshape(bp * n_mem, d), g_row(mem_norm_g[l]), wmem)
    mk_p = zmem[:, :MEM_HEADS * HEAD_DIM].reshape(bp, n_mem, MEM_HEADS, HEAD_DIM)
    mv_p = zmem[:, MEM_HEADS * HEAD_DIM:].reshape(bp, n_mem, MEM_HEADS, HEAD_DIM)
    om_p = _mem_attn(zp.reshape(bp, seq, D_Z), mk_p, mv_p, seqs_per_step=1, rows_per_step=min(512, seq))
    y_prompt = _merge(xp2, omla_p, od_p, om_p.reshape(bp * seq, -1), zp, wout, fg).reshape(bp, seq, d)

    xs2 = x_sample.reshape(nb * t_new, d)
    tm_s = min(256, nb * t_new)
    pos_s = past + (jnp.arange(tm_s, dtype=I32) % t_new)
    cos_s, sin_s = _rope_tables(pos_s)
    zs, (ckv_s, krope_s, ki_s, kd_s, vd_s, _, _, _, _, _, qcat_s) = branch(xs2, cos_s, sin_s, F32)
    wi_s = zs[:, C_WI:C_WI + IDX_HEADS].reshape(nb, t_new, IDX_HEADS // 2, 2)
    rows_i = (IDX_HEADS // 2) * t_new
    w_even = jnp.transpose(wi_s[..., 0], (0, 2, 1)).reshape(nb, rows_i, 1)
    w_odd = jnp.transpose(wi_s[..., 1], (0, 2, 1)).reshape(nb, rows_i, 1)
    qi_eo = jnp.transpose(zs[:, C_QI:C_QI + IDX_HEADS * IDX_DIM].reshape(nb, t_new, IDX_HEADS // 2, 2, IDX_DIM),
                          (0, 3, 2, 1, 4)).reshape(nb, 2, rows_i, IDX_DIM)
    krt_pool = jnp.swapaxes(cache_mla_krope, 2, 3)
    kit_pool = jnp.swapaxes(cache_idx_k, 2, 3)
    scores, omla_s = _sample_mla(page_table, qcat_s, qi_eo, w_even, w_odd, ckv_s, krope_s, ki_s, wv,
                                 cache_mla_ckv, krt_pool, kit_pool, nb=nb, t_new=t_new, c_pages=c_pages)
    thr, cut = _sample_thr(scores, topk=topk_s)
    od_s = _sample_dsa(page_table, t5_bias, zs, scores, thr, cut, kd_s, vd_s, cache_dsa_k, cache_dsa_v,
                       nb=nb, t_new=t_new, c_pages=c_pages)
    om_s = _mem_attn(zs.reshape(nb, t_new, D_Z), cache_mem_k[l], cache_mem_v[l],
                     seqs_per_step=min(4, nb), rows_per_step=t_new)
    y_sample = _merge(xs2, omla_s, od_s, om_s.reshape(nb * t_new, -1), zs, wout, fg).reshape(nb, t_new, d)

    st = lambda a, *shape: a.reshape((1,) + shape)
    return (y_prompt, y_sample,
            st(ckv_p, bp, seq, KV_LORA), st(krope_p, bp, seq, QK_ROPE),
            st(kd_p, bp, seq, DSA_KV_HEADS, HEAD_DIM), st(vd_p, bp, seq, DSA_KV_HEADS, HEAD_DIM),
            st(ki_p, bp, seq, IDX_DIM), st(mk_p, bp, n_mem, MEM_HEADS, HEAD_DIM), st(mv_p, bp, n_mem, MEM_HEADS, HEAD_DIM),
            st(ckv_s, nb, t_new, KV_LORA), st(krope_s, nb, t_new, QK_ROPE),
            st(kd_s, nb, t_new, DSA_KV_HEADS, HEAD_DIM), st(vd_s, nb, t_new, DSA_KV_HEADS, HEAD_DIM),
            st(ki_s, nb, t_new, IDX_DIM))
```

```python
import functools
import math

import numpy as np
import jax
import jax.numpy as jnp
from jax import lax
from jax.experimental import pallas as pl
from jax.experimental.pallas import tpu as pltpu

F32 = jnp.float32
BF16 = jnp.bfloat16
I32 = jnp.int32

HEAD_DIM = 128
MLA_HEADS = 8
DSA_HEADS = 4
DSA_KV_HEADS = 2
MEM_HEADS = 4
Q_LORA = 512
KV_LORA = 256
QK_ROPE = 64
IDX_HEADS = 16
IDX_DIM = 64
TOPK_MAX = 256
PAGE_SIZE = 128
NUM_BUCKETS = 32
ROPE_THETA = 10000.0
MLA_SCALE = (HEAD_DIM + QK_ROPE) ** -0.5
MLA_EXP2 = MLA_SCALE * math.log2(math.e)
MLA_GROUPS = 2
DSA_SCALE = HEAD_DIM ** -0.5
MEM_SCALE = HEAD_DIM ** -0.5
IDX_W_SCALE = (IDX_DIM ** -0.5) * (IDX_HEADS ** -0.5)
EPS = 1e-6
NEG_INF = float("-inf")
F32_MAX = float(np.finfo(np.float32).max)

C_Q, C_KV, C_KRKI, C_WI, C_GMLA, C_QD, C_KD, C_VD, C_QI, C_GDSA, C_QM, C_GMEM, D_Z = (
    0, 512, 768, 896, 1024, 2048, 2560, 2816, 3072, 4096, 4608, 5120, 5632)

T5_BUCKET_START = tuple(range(17)) + (19, 21, 24, 27, 31, 35, 40, 46, 52, 59, 67, 77, 87, 99, 113)
T5_FAR = T5_BUCKET_START[-1]

VMEM_LIMIT = 56 * 1024 * 1024


def _cparams(sem):
    return pltpu.CompilerParams(dimension_semantics=sem, vmem_limit_bytes=VMEM_LIMIT)


def _dot_nt(a, b):
    return lax.dot_general(a, b, (((1,), (1,)), ((), ())), preferred_element_type=F32)


def _dot(a, b):
    return jnp.dot(a, b, preferred_element_type=F32)


def _rms(x, g):
    return x * lax.rsqrt(jnp.mean(x * x, axis=-1, keepdims=True) + EPS) * g


def _inproj_kernel(x_ref, g_ref, w_ref, z_ref, h_ref):
    @pl.when(pl.program_id(1) == 0)
    def _():
        h_ref[...] = _rms(x_ref[...], g_ref[...]).astype(BF16)

    z_ref[...] = _dot(h_ref[...], w_ref[...])


def _in_proj(x, g, w, tn=512):
    n, d = x.shape
    m = w.shape[1]
    tm = min(1024, n)
    return pl.pallas_call(
        _inproj_kernel,
        out_shape=jax.ShapeDtypeStruct((n, m), F32),
        grid=(n // tm, m // tn),
        in_specs=[pl.BlockSpec((tm, d), lambda i, j: (i, 0)),
                  pl.BlockSpec((1, d), lambda i, j: (0, 0)),
                  pl.BlockSpec((d, tn), lambda i, j: (0, j))],
        out_specs=pl.BlockSpec((tm, tn), lambda i, j: (i, j)),
        scratch_shapes=[pltpu.VMEM((tm, d), BF16)],
        compiler_params=_cparams(("parallel", "arbitrary")),
        name="in_proj",
    )(x, g, w)


def _post_kernel(cq_ref, ckv_ref, krki_ref, kd_ref, vd_ref, cos_ref, sin_ref, qag_ref, kvag_ref, idxg_ref,
                 wuq_ref, wk_ref,
                 ckv_o, krope_o, ki_o, kd_o, vd_o, kcat_o, kip0_o, kip1_o, kdb_o, vdb_o, qcat_o):
    tm = cq_ref.shape[0]
    lane = lax.broadcasted_iota(I32, (tm, 128), 1)
    hi = lane >= 64
    second_half = (lane % 64) >= 32
    cos_t = cos_ref[...]
    sin_t = sin_ref[...]

    def rope128(x):
        swapped = jnp.where(second_half, pltpu.roll(x, 32, 1), pltpu.roll(x, 96, 1))
        return x * cos_t + swapped * sin_t

    ckv_n = _rms(ckv_ref[...], kvag_ref[...])
    ckv_o[...] = ckv_n

    krki = krki_ref[...]
    rot = rope128(krki)
    krope_o[...] = rot[:, :QK_ROPE]
    ms = jnp.sum(jnp.where(hi, krki * krki, 0.0), axis=-1, keepdims=True) * (1.0 / IDX_DIM)
    ki_hi = krki * lax.rsqrt(ms + EPS) * idxg_ref[...]
    ki_lo = pltpu.roll(ki_hi, 64, 1)
    ki_o[...] = ki_lo[:, :IDX_DIM]
    kip0_o[...] = jnp.where(hi, 0.0, ki_lo).astype(BF16)
    kip1_o[...] = jnp.where(hi, ki_hi, 0.0).astype(BF16)
    kcat_o[:, :KV_LORA] = ckv_n.astype(BF16)
    kcat_o[:, KV_LORA:] = jnp.where(hi, 0.0, rot).astype(BF16)

    kd = kd_ref[...]
    vd = vd_ref[...]
    kd_o[...] = kd
    vd_o[...] = vd
    kdb_o[...] = kd.astype(BF16)
    vdb_o[...] = vd.astype(BF16)

    cqn = _rms(cq_ref[...], qag_ref[...]).astype(BF16)
    qall = _dot(cqn, wuq_ref[...])
    nope_w = MLA_HEADS * HEAD_DIM
    for h in range(MLA_HEADS):
        qn = qall[:, h * HEAD_DIM:(h + 1) * HEAD_DIM].astype(BF16)
        qcat_o[h, :, :KV_LORA] = _dot(qn, wk_ref[h]).astype(qcat_o.dtype)
    for c in range(MLA_HEADS // 2):
        rc = rope128(qall[:, nope_w + 128 * c:nope_w + 128 * (c + 1)])
        qcat_o[2 * c, :, KV_LORA:] = jnp.where(hi, 0.0, rc).astype(qcat_o.dtype)
        qcat_o[2 * c + 1, :, KV_LORA:] = jnp.where(hi, 0.0, pltpu.roll(rc, 64, 1)).astype(qcat_o.dtype)


def _post(z, cos_t, sin_t, qag, kvag, idxg128, wuq, wk, q_dtype):
    n = z.shape[0]
    tm = min(256, n)
    nt = cos_t.shape[0] // tm
    row = lambda c, w: pl.BlockSpec((tm, w), lambda i, c=c, w=w: (i, c // w))
    full2 = lambda a: pl.BlockSpec(a.shape, lambda i: (0, 0))
    full3 = lambda a: pl.BlockSpec(a.shape, lambda i: (0, 0, 0))
    tab = pl.BlockSpec((tm, 128), lambda i: (i % nt, 0))
    out_shapes = [
        jax.ShapeDtypeStruct((n, KV_LORA), F32), jax.ShapeDtypeStruct((n, QK_ROPE), F32),
        jax.ShapeDtypeStruct((n, IDX_DIM), F32), jax.ShapeDtypeStruct((n, 256), F32),
        jax.ShapeDtypeStruct((n, 256), F32), jax.ShapeDtypeStruct((n, 384), BF16),
        jax.ShapeDtypeStruct((n, 128), BF16), jax.ShapeDtypeStruct((n, 128), BF16),
        jax.ShapeDtypeStruct((n, 256), BF16), jax.ShapeDtypeStruct((n, 256), BF16),
        jax.ShapeDtypeStruct((MLA_HEADS, n, 384), q_dtype)]
    o2 = lambda w: pl.BlockSpec((tm, w), lambda i: (i, 0))
    out_specs = [o2(KV_LORA), o2(QK_ROPE), o2(IDX_DIM), o2(256), o2(256), o2(384), o2(128), o2(128), o2(256),
                 o2(256), pl.BlockSpec((MLA_HEADS, tm, 384), lambda i: (0, i, 0))]
    return pl.pallas_call(
        _post_kernel,
        out_shape=out_shapes,
        grid=(n // tm,),
        in_specs=[row(C_Q, 512), row(C_KV, 256), row(C_KRKI, 128), row(C_KD, 256), row(C_VD, 256), tab, tab,
                  full2(qag), full2(kvag), full2(idxg128), full2(wuq), full3(wk)],
        out_specs=out_specs,
        compiler_params=_cparams(("parallel",)),
        name="post",
    )(z, z, z, z, z, cos_t, sin_t, qag, kvag, idxg128, wuq, wk)


def _top16(x):
    return lax.bitcast_convert_type(lax.bitcast_convert_type(x, I32) & np.int32(-65536), F32)


def _count_lanes(acc):
    return jnp.sum(acc.astype(F32), axis=-1, keepdims=True).astype(I32)


def _topk_threshold(count_hi, count_f32, k, rows):
    sign = np.int32(-2 ** 31)
    key_neg_inf = np.int32(-2139095041)

    def to_float(u):
        key = u ^ sign
        bits = jnp.where(key >= 0, key, key ^ np.int32(0x7FFFFFFF))
        return key, lax.bitcast_convert_type(bits, F32)

    def step(b, prefix, done, coarse):
        cand = prefix | jnp.left_shift(np.int32(1), jnp.asarray(31 - b, I32))
        key, cand_f = to_float(cand)
        n = count_hi(_top16(cand_f)) if coarse else count_f32(cand_f)
        ok = (n >= k) | (key <= key_neg_inf)
        return jnp.where(ok, cand, prefix), done | jnp.where(n == k, 1, 0)

    zero = jnp.zeros((rows, 1), I32)
    prefix, done = lax.fori_loop(0, 16, lambda b, c: step(b, c[0], c[1], True), (zero, zero))

    def fine(c):
        b, prefix, done, _ = c
        prefix, done = step(b, prefix, done, False)
        return b + 1, prefix, done, jnp.min(done)

    _, prefix, done, _ = lax.while_loop(lambda c: (c[0] < 32) & (c[3] == 0), fine,
                                        (jnp.int32(16), prefix, done, jnp.min(done)))
    key, thr = to_float(prefix)
    return jnp.where(key < key_neg_inf, NEG_INF, thr), done


def _tie_cutoff(count_eq_before, need, nbits, rows):
    def step(b, p):
        cand = p | jnp.left_shift(np.int32(1), jnp.asarray(nbits - 1 - b, I32))
        return jnp.where(count_eq_before(cand) < need, cand, p)
    return lax.fori_loop(0, nbits, step, jnp.zeros((rows, 1), I32))


def _t5_bias_tile(dist, t5_ref, heads):
    ge = [dist >= T5_BUCKET_START[b] for b in range(1, NUM_BUCKETS)]
    out = []
    for h in heads:
        bias = jnp.full(dist.shape, t5_ref[0, h], F32)
        for b in range(1, NUM_BUCKETS):
            bias = jnp.where(ge[b - 1], t5_ref[b, h], bias)
        out.append(bias)
    return out


def _softmax_update(s, m, l):
    m_new = jnp.maximum(m, jnp.max(s, axis=-1, keepdims=True))
    m_use = jnp.where(m_new == NEG_INF, 0.0, m_new)
    alpha = jnp.exp(m - m_use)
    p = jnp.exp(s - m_use)
    return m_new, alpha * l + jnp.sum(p, axis=-1, keepdims=True), alpha, p


def _prompt_attn_kernel(t5_ref, qcat_ref, qd_ref, qi_ref, wi_ref, kcat_ref, kip0_ref, kip1_ref, kd_ref, vd_ref,
                        wv_ref, omla_ref, od_ref, sc_ref, hi_ref, acc_ref, accd_ref, band_ref, *, t, topk):
    i = pl.program_id(1)
    rows_m = MLA_HEADS * t
    row = lax.broadcasted_iota(I32, (t, t), 0)
    col = lax.broadcasted_iota(I32, (t, t), 1)
    vis = col <= row

    @pl.when((pl.program_id(0) == 0) & (i == 0))
    def _():
        left = _t5_bias_tile(row - col + t, t5_ref, range(DSA_HEADS))
        diag = _t5_bias_tile(row - col, t5_ref, range(DSA_HEADS))
        for h in range(DSA_HEADS):
            band_ref[h, :, :t] = left[h]
            band_ref[h, :, t:] = diag[h]

    qall = qcat_ref[...].reshape(rows_m, 384)
    qi_b = qi_ref[...].astype(BF16)
    wsc = wi_ref[...] * IDX_W_SCALE
    acc_ref[...] = jnp.zeros_like(acc_ref)

    def kv_step(j, carry, masked):
        m, l = carry
        start = pl.multiple_of(j * t, t)
        kc = kcat_ref[pl.ds(start, t), :]
        rg = rows_m // MLA_GROUPS
        m_new, l_new = [], []
        for g in range(MLA_GROUPS):
            rows = slice(g * rg, (g + 1) * rg)
            s = _dot_nt(qall[rows], kc)
            if masked:
                s = jnp.where(vis[None], s.reshape(rg // t, t, t), NEG_INF).reshape(rg, t)
            m_g = jnp.maximum(m[g], jnp.max(s, axis=-1, keepdims=True))
            alpha = jnp.exp2((m[g] - m_g) * MLA_EXP2)
            p = jnp.exp2((s - m_g) * MLA_EXP2)
            l_new.append(alpha * l[g] + jnp.sum(p, axis=-1, keepdims=True))
            m_new.append(m_g)
            acc_ref[rows, :] = acc_ref[rows, :] * alpha + _dot(p.astype(BF16), kc[:, :KV_LORA])
        m_new, l_new = tuple(m_new), tuple(l_new)
        kp = (kip0_ref[pl.ds(start, t), :], kip1_ref[pl.ds(start, t), :])
        sc = jnp.zeros((t, t), F32)
        for h in range(IDX_HEADS):
            d = _dot_nt(qi_b[:, 128 * (h // 2):128 * (h // 2 + 1)], kp[h % 2])
            sc = sc + jnp.maximum(d, 0.0) * wsc[:, h:h + 1]
        if masked:
            sc = jnp.where(vis, sc, NEG_INF)
        sc_ref[j] = sc
        hi_ref[j] = _top16(sc).astype(BF16)
        return m_new, l_new

    rows_g = rows_m // MLA_GROUPS
    init = (tuple(jnp.full((rows_g, 1), NEG_INF, F32) for _ in range(MLA_GROUPS)),
            tuple(jnp.zeros((rows_g, 1), F32) for _ in range(MLA_GROUPS)))
    carry = lax.fori_loop(0, i, functools.partial(kv_step, masked=False), init)
    m_mla, l_mla = kv_step(i, carry, masked=True)

    lanes = t // 128
    col_t = lax.broadcasted_iota(I32, (t, t), 1)

    def count_tiles(pred, dtype):
        def body(j, acc):
            hit = jnp.where(pred(j), jnp.ones((), dtype), jnp.zeros((), dtype))
            for u in range(lanes):
                acc = acc + hit[:, u * 128:(u + 1) * 128]
            return acc
        return _count_lanes(lax.fori_loop(0, i + 1, body, jnp.zeros((t, 128), dtype)))

    def count_hi(c):
        cb = c.astype(BF16)
        return count_tiles(lambda j: hi_ref[j] >= cb, BF16)

    thr, done = _topk_threshold(count_hi, lambda c: count_tiles(lambda j: sc_ref[j] >= c, I32), topk, t)
    thr = jnp.maximum(thr, -F32_MAX)

    @pl.when(jnp.min(done) == 0)
    def _():
        need = topk - count_tiles(lambda j: sc_ref[j] > thr, I32)
        cut = _tie_cutoff(lambda p: count_tiles(lambda j: (sc_ref[j] == thr) & (j * t + col_t < p), I32),
                          need, max(1, (sc_ref.shape[0] * t - 1).bit_length()), t)

        def drop(j, _):
            x = sc_ref[j]
            sc_ref[j] = jnp.where((x == thr) & (j * t + col_t > cut), NEG_INF, x)
            return 0
        lax.fori_loop(0, i + 1, drop, 0)

    qd_b = qd_ref[...].astype(BF16)
    qg = [jnp.concatenate([qd_b[:, (2 * g) * 128:(2 * g + 1) * 128],
                           qd_b[:, (2 * g + 1) * 128:(2 * g + 2) * 128]], axis=0) for g in range(DSA_KV_HEADS)]
    accd_ref[...] = jnp.zeros_like(accd_ref)

    def dsa_step(j, carry, kind):
        start = pl.multiple_of(j * t, t)
        sel = sc_ref[j] >= thr
        new = []
        for g in range(DSA_KV_HEADS):
            m, l = carry[g]
            kd = kd_ref[pl.ds(start, t), g * 128:(g + 1) * 128]
            vd = vd_ref[pl.ds(start, t), g * 128:(g + 1) * 128]
            s = _dot_nt(qg[g], kd) * DSA_SCALE
            parts = []
            for r in range(2):
                h = 2 * g + r
                if kind == "far":
                    b = t5_ref[NUM_BUCKETS - 1, h]
                elif kind == "left":
                    b = band_ref[h, :, :t]
                else:
                    b = band_ref[h, :, t:]
                parts.append(jnp.where(sel, s[r * t:(r + 1) * t] + b, NEG_INF))
            s = jnp.concatenate(parts, axis=0)
            m_new, l_new, alpha, p = _softmax_update(s, m, l)
            accd_ref[g] = accd_ref[g] * alpha + _dot(p.astype(BF16), vd)
            new.append((m_new, l_new))
        return tuple(new)

    init_d = tuple((jnp.full((2 * t, 1), NEG_INF, F32), jnp.zeros((2 * t, 1), F32)) for _ in range(DSA_KV_HEADS))
    carry_d = lax.fori_loop(0, jnp.maximum(i - 1, 0), functools.partial(dsa_step, kind="far"), init_d)
    carry_d = lax.fori_loop(jnp.maximum(i - 1, 0), i, functools.partial(dsa_step, kind="left"), carry_d)
    carry_d = dsa_step(i, carry_d, "diag")

    for g in range(DSA_KV_HEADS):
        o = accd_ref[g] / carry_d[g][1]
        for r in range(2):
            h = 2 * g + r
            od_ref[:, h * 128:(h + 1) * 128] = o[r * t:(r + 1) * t]
    heads_g = MLA_HEADS // MLA_GROUPS
    for g in range(MLA_GROUPS):
        lat = (acc_ref[g * rows_g:(g + 1) * rows_g, :] / l_mla[g]).astype(BF16)
        for hh in range(heads_g):
            h = g * heads_g + hh
            omla_ref[:, h * 128:(h + 1) * 128] = _dot(lat[hh * t:(hh + 1) * t], wv_ref[h])


def _prompt_attn(t5, qcat8, z, kcat, kip0, kip1, kdb, vdb, wv, *, batch, seq, t, topk):
    assert t >= T5_FAR and seq % t == 0 and seq // 128 <= 256
    n = batch * seq
    nq = seq // t
    qrow = lambda c, w: pl.BlockSpec((t, w), lambda b, i, c=c, w=w: (b * nq + i, c // w))
    kfull = lambda w: pl.BlockSpec((seq, w), lambda b, i: (b, 0))
    kern = functools.partial(_prompt_attn_kernel, t=t, topk=topk)
    return pl.pallas_call(
        kern,
        out_shape=[jax.ShapeDtypeStruct((n, MLA_HEADS * HEAD_DIM), F32),
                   jax.ShapeDtypeStruct((n, DSA_HEADS * HEAD_DIM), F32)],
        grid=(batch, nq),
        in_specs=[pl.BlockSpec(memory_space=pltpu.SMEM),
                  pl.BlockSpec((MLA_HEADS, t, 384), lambda b, i: (0, b * nq + i, 0)),
                  qrow(C_QD, 512), qrow(C_QI, 1024), qrow(C_WI, 128),
                  kfull(384), kfull(128), kfull(128), kfull(256), kfull(256),
                  pl.BlockSpec(wv.shape, lambda b, i: (0, 0, 0))],
        out_specs=[pl.BlockSpec((t, MLA_HEADS * HEAD_DIM), lambda b, i: (b * nq + i, 0)),
                   pl.BlockSpec((t, DSA_HEADS * HEAD_DIM), lambda b, i: (b * nq + i, 0))],
        scratch_shapes=[pltpu.VMEM((nq, t, t), F32),
                        pltpu.VMEM((nq, t, t), BF16),
                        pltpu.VMEM((MLA_HEADS * t, KV_LORA), F32),
                        pltpu.VMEM((DSA_KV_HEADS, 2 * t, HEAD_DIM), F32),
                        pltpu.VMEM((DSA_HEADS, t, 2 * t), F32)],
        compiler_params=_cparams(("arbitrary", "arbitrary")),
        name="prompt_attn",
    )(t5, qcat8, z, z, z, kcat, kip0, kip1, kdb, vdb, wv)


def _pad_rows(x, rows):
    return jnp.concatenate([x, jnp.zeros((rows - x.shape[0], x.shape[1]), x.dtype)], axis=0)


N_SLOTS = 3


def _chunk_stream(n, nb, nch, chunk_copies):
    total = nb * nch

    def run(g, action):
        gc = jnp.minimum(g, total - 1)
        for cp in chunk_copies(gc // nch, gc % nch, g % N_SLOTS):
            getattr(cp, action)()

    def prologue():
        @pl.when(n == 0)
        def _():
            for g in range(N_SLOTS - 1):
                run(g, "start")

    def advance(c):
        g = n * nch + c
        run(g + N_SLOTS - 1, "start")
        run(g, "wait")
        return g % N_SLOTS

    def epilogue():
        @pl.when(n == nb - 1)
        def _():
            for g in range(total, total + N_SLOTS - 1):
                run(g, "wait")

    return prologue, advance, epilogue


def _sample_mla_kernel(pt_ref, qcat_ref, qi_ref, we_ref, wo_ref, ckvn_ref, krn_ref, kin_ref, wv_ref,
                       ckv_hbm, krt_hbm, kit_hbm, sc_ref, omla_ref, ckv_buf, krt_buf, kit_buf, sem, *, c_pages, nb):
    n = pl.program_id(0)
    nch = pt_ref.shape[1] // c_pages
    cw = c_pages * PAGE_SIZE
    past = nch * cw
    t_new = ckvn_ref.shape[0]
    rows_m = MLA_HEADS * t_new

    def chunk_copies(seq, c, slot):
        cps = []
        for u in range(c_pages):
            page = pt_ref[seq, c * c_pages + u]
            rows = pl.ds(u * PAGE_SIZE, PAGE_SIZE)
            cps.append(pltpu.make_async_copy(ckv_hbm.at[0, page], ckv_buf.at[slot, rows, :], sem.at[0, slot]))
            cps.append(pltpu.make_async_copy(krt_hbm.at[0, page], krt_buf.at[slot, :, rows], sem.at[1, slot]))
            cps.append(pltpu.make_async_copy(kit_hbm.at[0, page], kit_buf.at[slot, :, rows], sem.at[2, slot]))
        return cps

    prologue, advance, epilogue = _chunk_stream(n, nb, nch, chunk_copies)
    prologue()

    q = qcat_ref[...].reshape(rows_m, 384).astype(BF16)
    ql = q[:, :KV_LORA]
    qr = q[:, KV_LORA:KV_LORA + QK_ROPE]
    rows_i = qi_ref.shape[1]
    q_idx = qi_ref[...].reshape(2 * rows_i, IDX_DIM).astype(BF16)
    w_idx = jnp.concatenate([we_ref[...], wo_ref[...]], axis=0) * IDX_W_SCALE

    def idx_scores(dots):
        d = jnp.maximum(dots, 0.0) * w_idx
        return jnp.sum(d.reshape(IDX_HEADS, t_new, d.shape[-1]), axis=0)

    def chunk(c, carry):
        m, l, acc = carry
        slot = advance(c)
        ck = ckv_buf[slot].astype(BF16)
        s = (_dot_nt(ql, ck) + _dot(qr, krt_buf[slot].astype(BF16))) * MLA_SCALE
        m, l, alpha, p = _softmax_update(s, m, l)
        acc = acc * alpha + _dot(p.astype(BF16), ck)
        kit = kit_buf[slot].astype(BF16)
        sc_ref[:, pl.ds(pl.multiple_of(c * cw, cw), cw)] = idx_scores(_dot(q_idx, kit))
        return m, l, acc

    init = (jnp.full((rows_m, 1), NEG_INF, F32), jnp.zeros((rows_m, 1), F32), jnp.zeros((rows_m, KV_LORA), F32))
    m, l, acc = lax.fori_loop(0, nch, chunk, init)
    epilogue()

    tok = lax.broadcasted_iota(I32, (t_new, PAGE_SIZE), 0)
    col = lax.broadcasted_iota(I32, (t_new, PAGE_SIZE), 1)
    vis = col <= tok
    ck = _pad_rows(ckvn_ref[...], PAGE_SIZE).astype(BF16)
    kr = _pad_rows(krn_ref[...], PAGE_SIZE).astype(BF16)
    ki = _pad_rows(kin_ref[...], PAGE_SIZE).astype(BF16)
    s = (_dot_nt(ql, ck) + _dot_nt(qr, kr)) * MLA_SCALE
    s = jnp.where(vis[None], s.reshape(MLA_HEADS, t_new, PAGE_SIZE), NEG_INF).reshape(rows_m, PAGE_SIZE)
    m, l, alpha, p = _softmax_update(s, m, l)
    acc = acc * alpha + _dot(p.astype(BF16), ck)
    sc_ref[:, past:] = jnp.where(vis, idx_scores(_dot_nt(q_idx, ki)), NEG_INF)
    lat = (acc / l).astype(BF16)
    for h in range(MLA_HEADS):
        omla_ref[:, h * 128:(h + 1) * 128] = _dot(lat[h * t_new:(h + 1) * t_new], wv_ref[h])


def _sample_mla(page_table, qcat8, qi_eo, w_even, w_odd, ckv_new, kr_new, ki_new, wv, ckv_pool, krt_pool, kit_pool,
                *, nb, t_new, c_pages):
    npages = page_table.shape[1]
    nch = npages // c_pages
    assert nch * c_pages == npages
    cw = c_pages * PAGE_SIZE
    width = npages * PAGE_SIZE + PAGE_SIZE
    rows_i = (IDX_HEADS // 2) * t_new
    any_spec = pl.BlockSpec(memory_space=pl.ANY)
    in_specs = [pl.BlockSpec((MLA_HEADS, t_new, 384), lambda n, pt: (0, n, 0)),
                pl.BlockSpec((None, 2, rows_i, IDX_DIM), lambda n, pt: (n, 0, 0, 0)),
                pl.BlockSpec((None, rows_i, 1), lambda n, pt: (n, 0, 0)),
                pl.BlockSpec((None, rows_i, 1), lambda n, pt: (n, 0, 0)),
                pl.BlockSpec((t_new, KV_LORA), lambda n, pt: (n, 0)),
                pl.BlockSpec((t_new, QK_ROPE), lambda n, pt: (n, 0)),
                pl.BlockSpec((t_new, IDX_DIM), lambda n, pt: (n, 0)),
                pl.BlockSpec(wv.shape, lambda n, pt: (0, 0, 0)),
                any_spec, any_spec, any_spec]
    grid_spec = pltpu.PrefetchScalarGridSpec(
        num_scalar_prefetch=1, grid=(nb,), in_specs=in_specs,
        out_specs=[pl.BlockSpec((None, t_new, width), lambda n, pt: (n, 0, 0)),
                   pl.BlockSpec((t_new, MLA_HEADS * HEAD_DIM), lambda n, pt: (n, 0))],
        scratch_shapes=[pltpu.VMEM((N_SLOTS, cw, KV_LORA), F32), pltpu.VMEM((N_SLOTS, QK_ROPE, cw), F32),
                        pltpu.VMEM((N_SLOTS, IDX_DIM, cw), F32), pltpu.SemaphoreType.DMA((3, N_SLOTS))])
    return pl.pallas_call(
        functools.partial(_sample_mla_kernel, c_pages=c_pages, nb=nb),
        out_shape=[jax.ShapeDtypeStruct((nb, t_new, width), F32),
                   jax.ShapeDtypeStruct((nb * t_new, MLA_HEADS * HEAD_DIM), F32)],
        grid_spec=grid_spec,
        compiler_params=_cparams(("arbitrary",)),
        name="sample_mla",
    )(page_table, qcat8, qi_eo, w_even, w_odd, ckv_new, kr_new, ki_new, wv, ckv_pool, krt_pool, kit_pool)


def _sample_thr_kernel(sc_ref, thr_ref, cut_ref, hi_ref, *, topk, chunk):
    nseq, t_new, width = sc_ref.shape
    rows = nseq * t_new
    lanes = chunk // 128
    col_c = lax.broadcasted_iota(I32, (rows, chunk), 1)

    def load(c_idx):
        return sc_ref[:, :, pl.ds(pl.multiple_of(c_idx * chunk, chunk), chunk)].reshape(rows, chunk)

    def fill(c_idx, _):
        hi_ref[:, pl.ds(pl.multiple_of(c_idx * chunk, chunk), chunk)] = _top16(load(c_idx)).astype(BF16)
        return 0
    lax.fori_loop(0, width // chunk, fill, 0)

    def count_chunks(pred, dtype):
        def body(c_idx, acc):
            hit = jnp.where(pred(c_idx), jnp.ones((), dtype), jnp.zeros((), dtype))
            for u in range(lanes):
                acc = acc + hit[:, u * 128:(u + 1) * 128]
            return acc
        return _count_lanes(lax.fori_loop(0, width // chunk, body, jnp.zeros((rows, 128), dtype)))

    def count_hi(c):
        cb = c.astype(BF16)
        return count_chunks(lambda ci: hi_ref[:, pl.ds(pl.multiple_of(ci * chunk, chunk), chunk)] >= cb, BF16)

    thr, done = _topk_threshold(count_hi, lambda c: count_chunks(lambda ci: load(ci) >= c, I32), topk, rows)
    thr = jnp.maximum(thr, -F32_MAX)
    thr_ref[...] = thr
    cut_ref[...] = jnp.full((rows, 1), np.iinfo(np.int32).max, I32)

    @pl.when(jnp.min(done) == 0)
    def _():
        need = topk - count_chunks(lambda ci: load(ci) > thr, I32)
        cut_ref[...] = _tie_cutoff(
            lambda p: count_chunks(lambda ci: (load(ci) == thr) & (ci * chunk + col_c < p), I32),
            need, max(1, (width - 1).bit_length()), rows)


def _sample_thr(scores, *, topk, nseq=8):
    nb, t_new, width = scores.shape
    nseq = min(nseq, nb)
    lanes = width // 128
    assert lanes <= 256
    chunk = 128 * max(d for d in range(1, 5) if lanes % d == 0)
    rows = nseq * t_new
    return pl.pallas_call(
        functools.partial(_sample_thr_kernel, topk=topk, chunk=chunk),
        out_shape=[jax.ShapeDtypeStruct((nb * t_new, 1), F32), jax.ShapeDtypeStruct((nb * t_new, 1), I32)],
        grid=(nb // nseq,),
        in_specs=[pl.BlockSpec((nseq, t_new, width), lambda i: (i, 0, 0))],
        out_specs=[pl.BlockSpec((rows, 1), lambda i: (i, 0)), pl.BlockSpec((rows, 1), lambda i: (i, 0))],
        scratch_shapes=[pltpu.VMEM((rows, width), BF16)],
        compiler_params=_cparams(("parallel",)),
        name="sample_thr",
    )(scores)


def _sample_dsa_kernel(pt_ref, t5_ref, qd_ref, sc_ref, thr_ref, cut_ref, kdn_ref, vdn_ref, kd_hbm, vd_hbm, od_ref,
                       kd_buf, vd_buf, sem, *, c_pages, nb):
    n = pl.program_id(0)
    nch = pt_ref.shape[1] // c_pages
    cw = c_pages * PAGE_SIZE
    past = nch * cw
    t_new = qd_ref.shape[0]

    def chunk_copies(seq, c, slot):
        cps = []
        for u in range(c_pages):
            page = pt_ref[seq, c * c_pages + u]
            rows = pl.ds(u * PAGE_SIZE, PAGE_SIZE)
            for g in range(DSA_KV_HEADS):
                cps.append(pltpu.make_async_copy(kd_hbm.at[0, page, :, g, :], kd_buf.at[slot, g, rows, :], sem.at[0, slot]))
                cps.append(pltpu.make_async_copy(vd_hbm.at[0, page, :, g, :], vd_buf.at[slot, g, rows, :], sem.at[1, slot]))
        return cps

    prologue, advance, epilogue = _chunk_stream(n, nb, nch, chunk_copies)
    prologue()

    qd_b = qd_ref[...].astype(BF16)
    qg = [jnp.concatenate([qd_b[:, (2 * g) * 128:(2 * g + 1) * 128],
                           qd_b[:, (2 * g + 1) * 128:(2 * g + 2) * 128]], axis=0) for g in range(DSA_KV_HEADS)]
    thr = thr_ref[...]
    cut = cut_ref[...]

    def selected(start, width):
        x = sc_ref[:, pl.ds(start, width)]
        col = start + lax.broadcasted_iota(I32, (t_new, width), 1)
        return (x > thr) | ((x == thr) & (col <= cut))

    def attend(carry, k_of, v_of, sel, dist):
        if dist is not None:
            bias = _t5_bias_tile(dist, t5_ref, range(DSA_HEADS))
        new = []
        for g in range(DSA_KV_HEADS):
            m, l, acc = carry[g]
            s = _dot_nt(qg[g], k_of(g)) * DSA_SCALE
            parts = []
            for r in range(2):
                h = 2 * g + r
                b = bias[h] if dist is not None else t5_ref[NUM_BUCKETS - 1, h]
                parts.append(jnp.where(sel, s[r * t_new:(r + 1) * t_new] + b, NEG_INF))
            m_new, l_new, alpha, p = _softmax_update(jnp.concatenate(parts, axis=0), m, l)
            new.append((m_new, l_new, acc * alpha + _dot(p.astype(BF16), v_of(g))))
        return tuple(new)

    def chunk(c, carry, near):
        slot = advance(c)
        start = pl.multiple_of(c * cw, cw)
        dist = None
        if near:
            tok = lax.broadcasted_iota(I32, (t_new, cw), 0)
            col = lax.broadcasted_iota(I32, (t_new, cw), 1)
            dist = (past + tok) - (start + col)
        return attend(carry, lambda g: kd_buf[slot, g].astype(BF16), lambda g: vd_buf[slot, g].astype(BF16),
                      selected(start, cw), dist)

    init = tuple((jnp.full((2 * t_new, 1), NEG_INF, F32), jnp.zeros((2 * t_new, 1), F32),
                  jnp.zeros((2 * t_new, HEAD_DIM), F32)) for _ in range(DSA_KV_HEADS))
    carry = lax.fori_loop(0, nch - 1, functools.partial(chunk, near=False), init)
    carry = chunk(nch - 1, carry, near=True)
    epilogue()

    tok = lax.broadcasted_iota(I32, (t_new, PAGE_SIZE), 0)
    col = lax.broadcasted_iota(I32, (t_new, PAGE_SIZE), 1)
    kdn = _pad_rows(kdn_ref[...], PAGE_SIZE).astype(BF16)
    vdn = _pad_rows(vdn_ref[...], PAGE_SIZE).astype(BF16)
    carry = attend(carry, lambda g: kdn[:, g * 128:(g + 1) * 128], lambda g: vdn[:, g * 128:(g + 1) * 128],
                   selected(past, PAGE_SIZE), tok - col)
    for g in range(DSA_KV_HEADS):
        o = carry[g][2] / carry[g][1]
        for r in range(2):
            h = 2 * g + r
            od_ref[:, h * 128:(h + 1) * 128] = o[r * t_new:(r + 1) * t_new]


def _sample_dsa(page_table, t5, z, scores, thr, cut, kd_new, vd_new, kd_pool, vd_pool, *, nb, t_new, c_pages):
    npages = page_table.shape[1]
    nch = npages // c_pages
    assert nch * c_pages == npages and c_pages * PAGE_SIZE >= T5_FAR
    cw = c_pages * PAGE_SIZE
    width = scores.shape[-1]
    any_spec = pl.BlockSpec(memory_space=pl.ANY)
    in_specs = [pl.BlockSpec(memory_space=pltpu.SMEM),
                pl.BlockSpec((t_new, 512), lambda n, pt: (n, C_QD // 512)),
                pl.BlockSpec((None, t_new, width), lambda n, pt: (n, 0, 0)),
                pl.BlockSpec((t_new, 1), lambda n, pt: (n, 0)),
                pl.BlockSpec((t_new, 1), lambda n, pt: (n, 0)),
                pl.BlockSpec((t_new, 256), lambda n, pt: (n, 0)),
                pl.BlockSpec((t_new, 256), lambda n, pt: (n, 0)),
                any_spec, any_spec]
    grid_spec = pltpu.PrefetchScalarGridSpec(
        num_scalar_prefetch=1, grid=(nb,), in_specs=in_specs,
        out_specs=pl.BlockSpec((t_new, DSA_HEADS * HEAD_DIM), lambda n, pt: (n, 0)),
        scratch_shapes=[pltpu.VMEM((N_SLOTS, DSA_KV_HEADS, cw, HEAD_DIM), F32),
                        pltpu.VMEM((N_SLOTS, DSA_KV_HEADS, cw, HEAD_DIM), F32),
                        pltpu.SemaphoreType.DMA((2, N_SLOTS))])
    return pl.pallas_call(
        functools.partial(_sample_dsa_kernel, c_pages=c_pages, nb=nb),
        out_shape=jax.ShapeDtypeStruct((nb * t_new, DSA_HEADS * HEAD_DIM), F32),
        grid_spec=grid_spec,
        compiler_params=_cparams(("arbitrary",)),
        name="sample_dsa",
    )(page_table, t5, z, scores, thr, cut, kd_new, vd_new, kd_pool, vd_pool)


def _mem_attn_kernel(q_ref, k_ref, v_ref, o_ref):
    for s in range(q_ref.shape[0]):
        q = q_ref[s].astype(BF16)
        for h in range(MEM_HEADS):
            k = k_ref[s, :, h, :].astype(BF16)
            v = v_ref[s, :, h, :].astype(BF16)
            logits = _dot_nt(q[:, h * 128:(h + 1) * 128], k) * MEM_SCALE
            p = jnp.exp(logits - jnp.max(logits, axis=-1, keepdims=True))
            o = _dot(p.astype(BF16), v) / jnp.sum(p, axis=-1, keepdims=True)
            o_ref[s, :, h * 128:(h + 1) * 128] = o


def _mem_attn(z3, mem_k, mem_v, *, seqs_per_step, rows_per_step):
    groups, rows, _ = z3.shape
    n_mem = mem_k.shape[1]
    gs, rs = seqs_per_step, rows_per_step
    kv_spec = pl.BlockSpec((gs, n_mem, MEM_HEADS, HEAD_DIM), lambda a, b: (a, 0, 0, 0))
    return pl.pallas_call(
        _mem_attn_kernel,
        out_shape=jax.ShapeDtypeStruct((groups, rows, MEM_HEADS * HEAD_DIM), F32),
        grid=(groups // gs, rows // rs),
        in_specs=[pl.BlockSpec((gs, rs, 512), lambda a, b: (a, b, C_QM // 512)), kv_spec, kv_spec],
        out_specs=pl.BlockSpec((gs, rs, 512), lambda a, b: (a, b, 0)),
        compiler_params=_cparams(("parallel", "arbitrary")),
        name="mem_attn",
    )(z3, mem_k, mem_v)


def _mem_attn_rows_kernel(q_ref, k_ref, v_ref, o_ref):
    gs, t_new, _ = q_ref.shape
    nrow = k_ref.shape[1]
    rows = MEM_HEADS * t_new
    same_head = (lax.broadcasted_iota(I32, (rows, nrow), 0) // t_new
                 == lax.broadcasted_iota(I32, (rows, nrow), 1) % MEM_HEADS)
    for s in range(gs):
        q = q_ref[s]
        qs = jnp.concatenate([q[:, h * 128:(h + 1) * 128] for h in range(MEM_HEADS)], axis=0).astype(BF16)
        logits = jnp.where(same_head, _dot_nt(qs, k_ref[s].astype(BF16)) * MEM_SCALE, NEG_INF)
        p = jnp.exp(logits - jnp.max(logits, axis=-1, keepdims=True))
        o = _dot(p.astype(BF16), v_ref[s].astype(BF16)) / jnp.sum(p, axis=-1, keepdims=True)
        for h in range(MEM_HEADS):
            o_ref[s, :, h * 128:(h + 1) * 128] = o[h * t_new:(h + 1) * t_new]


def _mem_attn_rows(z3, mem_k, mem_v, *, seqs_per_step):
    groups, t_new, _ = z3.shape
    gs = seqs_per_step
    kv_spec = pl.BlockSpec((gs,) + mem_k.shape[1:], lambda a: (a, 0, 0))
    return pl.pallas_call(
        _mem_attn_rows_kernel,
        out_shape=jax.ShapeDtypeStruct((groups, t_new, MEM_HEADS * HEAD_DIM), F32),
        grid=(groups // gs,),
        in_specs=[pl.BlockSpec((gs, t_new, 512), lambda a: (a, 0, C_QM // 512)), kv_spec, kv_spec],
        out_specs=pl.BlockSpec((gs, t_new, 512), lambda a: (a, 0, 0)),
        compiler_params=_cparams(("parallel",)),
        name="mem_attn_rows",
    )(z3, mem_k, mem_v)


def _silu(g):
    return g * (1.0 / (1.0 + jnp.exp(-g)))


def _merge_kernel(x_ref, omla_ref, od_ref, om_ref, gmla_ref, gdsa_ref, gmem_ref, wout_ref, fg_ref, y_ref):
    o = jnp.concatenate([(omla_ref[...] * _silu(gmla_ref[...])).astype(BF16),
                         (od_ref[...] * _silu(gdsa_ref[...])).astype(BF16),
                         (om_ref[...] * _silu(gmem_ref[...])).astype(BF16)], axis=-1)
    xo = x_ref[...] + _dot(o, wout_ref[...])
    y_ref[...] = _rms(xo, fg_ref[...])


def _merge(x, omla, od, om, z, wout, fg):
    n, d = x.shape
    tm = min(256, n)
    row = lambda w, c=0: pl.BlockSpec((tm, w), lambda i, c=c, w=w: (i, c // w))
    return pl.pallas_call(
        _merge_kernel,
        out_shape=jax.ShapeDtypeStruct((n, d), F32),
        grid=(n // tm,),
        in_specs=[row(d), row(1024), row(512), row(512), row(1024, C_GMLA), row(512, C_GDSA), row(512, C_GMEM),
                  pl.BlockSpec(wout.shape, lambda i: (0, 0)), pl.BlockSpec((1, d), lambda i: (0, 0))],
        out_specs=row(d),
        compiler_params=_cparams(("parallel",)),
        name="merge",
    )(x, omla, od, om, z, z, z, wout, fg)


def _rope_tables(pos):
    half = QK_ROPE // 2
    inv_freq = jnp.exp(-math.log(ROPE_THETA) * jnp.arange(half, dtype=F32) / half)
    ang = pos.astype(F32)[:, None] * inv_freq[None, :]
    cos, sin = jnp.cos(ang), jnp.sin(ang)
    return jnp.tile(cos, (1, 4)), jnp.concatenate([-sin, sin, -sin, sin], axis=1)


def _permute_w_in(w):
    d = w.shape[0]
    seg = lambda a, b: w[:, a:b]
    return jnp.concatenate([
        seg(0, 512), seg(512, 768), seg(768, 832), seg(3904, 3968), seg(3968, 3984), jnp.zeros((d, 112), w.dtype),
        seg(832, 1856), seg(1856, 2368), seg(2368, 2624), seg(2624, 2880), seg(2880, 3904), seg(3984, 4496),
        seg(4496, 5008), seg(5008, 5520)], axis=1).astype(BF16)


def kernel(x_prompt, x_sample, mem_prompt, cache_mla_ckv, cache_mla_krope, cache_dsa_k, cache_dsa_v, cache_idx_k,
           cache_mem_k, cache_mem_v, page_table, t5_bias, norm_g, w_in, qa_norm_g, w_uq, kva_norm_g, w_ukv,
           idx_norm_g, mem_norm_g, w_mem_kv, w_out, final_norm_g, *, t_blk=256, c_pages=16):
    bp, seq, d = x_prompt.shape
    nb, t_new, _ = x_sample.shape
    n_mem = mem_prompt.shape[1]
    npages = page_table.shape[1]
    past = npages * PAGE_SIZE
    depth = norm_g.shape[0]
    assert depth == 1
    l = 0
    t_blk = min(t_blk, seq)
    c_pages = min(c_pages, npages // 2)
    topk_p = min(TOPK_MAX, seq // 4)
    topk_s = min(TOPK_MAX, (past + t_new) // 4)

    w_in_p = _permute_w_in(w_in[l])
    wuq = jnp.concatenate([w_uq[l][:, :, :HEAD_DIM].reshape(Q_LORA, -1),
                           w_uq[l][:, :, HEAD_DIM:].reshape(Q_LORA, -1)], axis=1).astype(BF16)
    wk = jnp.transpose(w_ukv[l][:, :, :HEAD_DIM], (1, 2, 0)).astype(BF16)
    wv = jnp.transpose(w_ukv[l][:, :, HEAD_DIM:], (1, 0, 2)).astype(BF16)
    wout = w_out[l].astype(BF16)
    wmem = w_mem_kv[l].astype(BF16)
    g_row = lambda g: g.reshape(1, -1)
    idxg128 = jnp.concatenate([jnp.zeros((1, 64), F32), g_row(idx_norm_g[l])], axis=1)
    fg = g_row(final_norm_g)

    def branch(x2, cos_t, sin_t, q_dtype):
        z = _in_proj(x2, g_row(norm_g[l]), w_in_p)
        outs = _post(z, cos_t, sin_t, g_row(qa_norm_g[l]), g_row(kva_norm_g[l]), idxg128, wuq, wk, q_dtype)
        return z, outs

    xp2 = x_prompt.reshape(bp * seq, d)
    cos_p, sin_p = _rope_tables(jnp.arange(seq, dtype=I32))
    zp, (ckv_p, krope_p, ki_p, kd_p, vd_p, kcat_p, kip0_p, kip1_p, kdb_p, vdb_p, qcat_p) = branch(xp2, cos_p, sin_p, BF16)
    omla_p, od_p = _prompt_attn(t5_bias, qcat_p, zp, kcat_p, kip0_p, kip1_p, kdb_p, vdb_p, wv,
                                batch=bp, seq=seq, t=t_blk, topk=topk_p)
    zmem = _in_proj(mem_prompt.reshape(bp * n_mem, d), g_row(mem_norm_g[l]), wmem)
    mk_p = zmem[:, :MEM_HEADS * HEAD_DIM].reshape(bp, n_mem, MEM_HEADS, HEAD_DIM)
    mv_p = zmem[:, MEM_HEADS * HEAD_DIM:].reshape(bp, n_mem, MEM_HEADS, HEAD_DIM)
    om_p = _mem_attn(zp.reshape(bp, seq, D_Z), mk_p, mv_p, seqs_per_step=1, rows_per_step=min(512, seq))
    y_prompt = _merge(xp2, omla_p, od_p, om_p.reshape(bp * seq, -1), zp, wout, fg).reshape(bp, seq, d)

    xs2 = x_sample.reshape(nb * t_new, d)
    tm_s = min(256, nb * t_new)
    pos_s = past + (jnp.arange(tm_s, dtype=I32) % t_new)
    cos_s, sin_s = _rope_tables(pos_s)
    zs, (ckv_s, krope_s, ki_s, kd_s, vd_s, _, _, _, _, _, qcat_s) = branch(xs2, cos_s, sin_s, F32)
    wi_s = zs[:, C_WI:C_WI + IDX_HEADS].reshape(nb, t_new, IDX_HEADS // 2, 2)
    rows_i = (IDX_HEADS // 2) * t_new
    w_even = jnp.transpose(wi_s[..., 0], (0, 2, 1)).reshape(nb, rows_i, 1)
    w_odd = jnp.transpose(wi_s[..., 1], (0, 2, 1)).reshape(nb, rows_i, 1)
    qi_eo = jnp.transpose(zs[:, C_QI:C_QI + IDX_HEADS * IDX_DIM].reshape(nb, t_new, IDX_HEADS // 2, 2, IDX_DIM),
                          (0, 3, 2, 1, 4)).reshape(nb, 2, rows_i, IDX_DIM)
    krt_pool = jnp.swapaxes(cache_mla_krope, 2, 3)
    kit_pool = jnp.swapaxes(cache_idx_k, 2, 3)
    scores, omla_s = _sample_mla(page_table, qcat_s, qi_eo, w_even, w_odd, ckv_s, krope_s, ki_s, wv,
                                 cache_mla_ckv, krt_pool, kit_pool, nb=nb, t_new=t_new, c_pages=c_pages)
    thr, cut = _sample_thr(scores, topk=topk_s)
    od_s = _sample_dsa(page_table, t5_bias, zs, scores, thr, cut, kd_s, vd_s, cache_dsa_k, cache_dsa_v,
                       nb=nb, t_new=t_new, c_pages=c_pages)
    mem_rows = lambda a: a.reshape(nb, a.shape[1] * MEM_HEADS, HEAD_DIM)
    om_s = _mem_attn_rows(zs.reshape(nb, t_new, D_Z), mem_rows(cache_mem_k[l]), mem_rows(cache_mem_v[l]),
                          seqs_per_step=min(8, nb))
    y_sample = _merge(xs2, omla_s, od_s, om_s.reshape(nb * t_new, -1), zs, wout, fg).reshape(nb, t_new, d)

    st = lambda a, *shape: a.reshape((1,) + shape)
    return (y_prompt, y_sample,
            st(ckv_p, bp, seq, KV_LORA), st(krope_p, bp, seq, QK_ROPE),
            st(kd_p, bp, seq, DSA_KV_HEADS, HEAD_DIM), st(vd_p, bp, seq, DSA_KV_HEADS, HEAD_DIM),
            st(ki_p, bp, seq, IDX_DIM), st(mk_p, bp, n_mem, MEM_HEADS, HEAD_DIM), st(mv_p, bp, n_mem, MEM_HEADS, HEAD_DIM),
            st(ckv_s, nb, t_new, KV_LORA), st(krope_s, nb, t_new, QK_ROPE),
            st(kd_s, nb, t_new, DSA_KV_HEADS, HEAD_DIM), st(vd_s, nb, t_new, DSA_KV_HEADS, HEAD_DIM),
            st(ki_s, nb, t_new, IDX_DIM))
```

```python
import functools
import math

import numpy as np
import jax
import jax.numpy as jnp
from jax import lax
from jax.experimental import pallas as pl
from jax.experimental.pallas import tpu as pltpu

F32 = jnp.float32
BF16 = jnp.bfloat16
I32 = jnp.int32

HEAD_DIM = 128
MLA_HEADS = 8
DSA_HEADS = 4
DSA_KV_HEADS = 2
MEM_HEADS = 4
Q_LORA = 512
KV_LORA = 256
QK_ROPE = 64
IDX_HEADS = 16
IDX_DIM = 64
TOPK_MAX = 256
PAGE_SIZE = 128
NUM_BUCKETS = 32
ROPE_THETA = 10000.0
MLA_SCALE = (HEAD_DIM + QK_ROPE) ** -0.5
MLA_EXP2 = MLA_SCALE * math.log2(math.e)
MLA_GROUPS = 4
DSA_SCALE = HEAD_DIM ** -0.5
MEM_SCALE = HEAD_DIM ** -0.5
IDX_W_SCALE = (IDX_DIM ** -0.5) * (IDX_HEADS ** -0.5)
EPS = 1e-6
NEG_INF = float("-inf")
F32_MAX = float(np.finfo(np.float32).max)

C_Q, C_KV, C_KRKI, C_WI, C_GMLA, C_QD, C_KD, C_VD, C_QI, C_GDSA, C_QM, C_GMEM, D_Z = (
    0, 512, 768, 896, 1024, 2048, 2560, 2816, 3072, 4096, 4608, 5120, 5632)

T5_BUCKET_START = tuple(range(17)) + (19, 21, 24, 27, 31, 35, 40, 46, 52, 59, 67, 77, 87, 99, 113)
T5_FAR = T5_BUCKET_START[-1]

VMEM_LIMIT = 56 * 1024 * 1024


def _cparams(sem):
    return pltpu.CompilerParams(dimension_semantics=sem, vmem_limit_bytes=VMEM_LIMIT)


def _dot_nt(a, b):
    return lax.dot_general(a, b, (((1,), (1,)), ((), ())), preferred_element_type=F32)


def _dot(a, b):
    return jnp.dot(a, b, preferred_element_type=F32)


def _rms(x, g):
    return x * lax.rsqrt(jnp.mean(x * x, axis=-1, keepdims=True) + EPS) * g


def _inproj_kernel(x_ref, g_ref, w_ref, z_ref, h_ref):
    @pl.when(pl.program_id(1) == 0)
    def _():
        h_ref[...] = _rms(x_ref[...], g_ref[...]).astype(BF16)

    z_ref[...] = _dot(h_ref[...], w_ref[...])


def _in_proj(x, g, w, tn=512):
    n, d = x.shape
    m = w.shape[1]
    tm = min(1024, n)
    return pl.pallas_call(
        _inproj_kernel,
        out_shape=jax.ShapeDtypeStruct((n, m), F32),
        grid=(n // tm, m // tn),
        in_specs=[pl.BlockSpec((tm, d), lambda i, j: (i, 0)),
                  pl.BlockSpec((1, d), lambda i, j: (0, 0)),
                  pl.BlockSpec((d, tn), lambda i, j: (0, j))],
        out_specs=pl.BlockSpec((tm, tn), lambda i, j: (i, j)),
        scratch_shapes=[pltpu.VMEM((tm, d), BF16)],
        compiler_params=_cparams(("parallel", "arbitrary")),
        name="in_proj",
    )(x, g, w)


def _post_kernel(cq_ref, ckv_ref, krki_ref, kd_ref, vd_ref, cos_ref, sin_ref, qag_ref, kvag_ref, idxg_ref,
                 wuq_ref, wk_ref,
                 ckv_o, krope_o, ki_o, kd_o, vd_o, kcat_o, kip0_o, kip1_o, kdb_o, vdb_o, qcat_o):
    tm = cq_ref.shape[0]
    lane = lax.broadcasted_iota(I32, (tm, 128), 1)
    hi = lane >= 64
    second_half = (lane % 64) >= 32
    cos_t = cos_ref[...]
    sin_t = sin_ref[...]

    def rope128(x):
        swapped = jnp.where(second_half, pltpu.roll(x, 32, 1), pltpu.roll(x, 96, 1))
        return x * cos_t + swapped * sin_t

    ckv_n = _rms(ckv_ref[...], kvag_ref[...])
    ckv_o[...] = ckv_n

    krki = krki_ref[...]
    rot = rope128(krki)
    krope_o[...] = rot[:, :QK_ROPE]
    ms = jnp.sum(jnp.where(hi, krki * krki, 0.0), axis=-1, keepdims=True) * (1.0 / IDX_DIM)
    ki_hi = krki * lax.rsqrt(ms + EPS) * idxg_ref[...]
    ki_lo = pltpu.roll(ki_hi, 64, 1)
    ki_o[...] = ki_lo[:, :IDX_DIM]
    kip0_o[...] = jnp.where(hi, 0.0, ki_lo).astype(BF16)
    kip1_o[...] = jnp.where(hi, ki_hi, 0.0).astype(BF16)
    kcat_o[:, :KV_LORA] = ckv_n.astype(BF16)
    kcat_o[:, KV_LORA:] = jnp.where(hi, 0.0, rot).astype(BF16)

    kd = kd_ref[...]
    vd = vd_ref[...]
    kd_o[...] = kd
    vd_o[...] = vd
    kdb_o[...] = kd.astype(BF16)
    vdb_o[...] = vd.astype(BF16)

    cqn = _rms(cq_ref[...], qag_ref[...]).astype(BF16)
    qall = _dot(cqn, wuq_ref[...])
    nope_w = MLA_HEADS * HEAD_DIM
    for h in range(MLA_HEADS):
        qn = qall[:, h * HEAD_DIM:(h + 1) * HEAD_DIM].astype(BF16)
        qcat_o[h, :, :KV_LORA] = _dot(qn, wk_ref[h]).astype(qcat_o.dtype)
    for c in range(MLA_HEADS // 2):
        rc = rope128(qall[:, nope_w + 128 * c:nope_w + 128 * (c + 1)])
        qcat_o[2 * c, :, KV_LORA:] = jnp.where(hi, 0.0, rc).astype(qcat_o.dtype)
        qcat_o[2 * c + 1, :, KV_LORA:] = jnp.where(hi, 0.0, pltpu.roll(rc, 64, 1)).astype(qcat_o.dtype)


def _post(z, cos_t, sin_t, qag, kvag, idxg128, wuq, wk, q_dtype):
    n = z.shape[0]
    tm = min(256, n)
    nt = cos_t.shape[0] // tm
    row = lambda c, w: pl.BlockSpec((tm, w), lambda i, c=c, w=w: (i, c // w))
    full2 = lambda a: pl.BlockSpec(a.shape, lambda i: (0, 0))
    full3 = lambda a: pl.BlockSpec(a.shape, lambda i: (0, 0, 0))
    tab = pl.BlockSpec((tm, 128), lambda i: (i % nt, 0))
    out_shapes = [
        jax.ShapeDtypeStruct((n, KV_LORA), F32), jax.ShapeDtypeStruct((n, QK_ROPE), F32),
        jax.ShapeDtypeStruct((n, IDX_DIM), F32), jax.ShapeDtypeStruct((n, 256), F32),
        jax.ShapeDtypeStruct((n, 256), F32), jax.ShapeDtypeStruct((n, 384), BF16),
        jax.ShapeDtypeStruct((n, 128), BF16), jax.ShapeDtypeStruct((n, 128), BF16),
        jax.ShapeDtypeStruct((n, 256), BF16), jax.ShapeDtypeStruct((n, 256), BF16),
        jax.ShapeDtypeStruct((MLA_HEADS, n, 384), q_dtype)]
    o2 = lambda w: pl.BlockSpec((tm, w), lambda i: (i, 0))
    out_specs = [o2(KV_LORA), o2(QK_ROPE), o2(IDX_DIM), o2(256), o2(256), o2(384), o2(128), o2(128), o2(256),
                 o2(256), pl.BlockSpec((MLA_HEADS, tm, 384), lambda i: (0, i, 0))]
    return pl.pallas_call(
        _post_kernel,
        out_shape=out_shapes,
        grid=(n // tm,),
        in_specs=[row(C_Q, 512), row(C_KV, 256), row(C_KRKI, 128), row(C_KD, 256), row(C_VD, 256), tab, tab,
                  full2(qag), full2(kvag), full2(idxg128), full2(wuq), full3(wk)],
        out_specs=out_specs,
        compiler_params=_cparams(("parallel",)),
        name="post",
    )(z, z, z, z, z, cos_t, sin_t, qag, kvag, idxg128, wuq, wk)


def _top16(x):
    return lax.bitcast_convert_type(lax.bitcast_convert_type(x, I32) & np.int32(-65536), F32)


def _count_lanes(acc):
    return jnp.sum(acc.astype(F32), axis=-1, keepdims=True).astype(I32)


def _topk_threshold(count_hi, count_f32, k, rows):
    sign = np.int32(-2 ** 31)
    key_neg_inf = np.int32(-2139095041)

    def to_float(u):
        key = u ^ sign
        bits = jnp.where(key >= 0, key, key ^ np.int32(0x7FFFFFFF))
        return key, lax.bitcast_convert_type(bits, F32)

    def step(b, prefix, done, coarse):
        cand = prefix | jnp.left_shift(np.int32(1), jnp.asarray(31 - b, I32))
        key, cand_f = to_float(cand)
        n = count_hi(_top16(cand_f)) if coarse else count_f32(cand_f)
        ok = (n >= k) | (key <= key_neg_inf)
        return jnp.where(ok, cand, prefix), done | jnp.where(n == k, 1, 0)

    zero = jnp.zeros((rows, 1), I32)
    prefix, done = lax.fori_loop(0, 16, lambda b, c: step(b, c[0], c[1], True), (zero, zero))

    def fine(c):
        b, prefix, done, _ = c
        prefix, done = step(b, prefix, done, False)
        return b + 1, prefix, done, jnp.min(done)

    _, prefix, done, _ = lax.while_loop(lambda c: (c[0] < 32) & (c[3] == 0), fine,
                                        (jnp.int32(16), prefix, done, jnp.min(done)))
    key, thr = to_float(prefix)
    return jnp.where(key < key_neg_inf, NEG_INF, thr), done


def _tie_cutoff(count_eq_before, need, nbits, rows):
    def step(b, p):
        cand = p | jnp.left_shift(np.int32(1), jnp.asarray(nbits - 1 - b, I32))
        return jnp.where(count_eq_before(cand) < need, cand, p)
    return lax.fori_loop(0, nbits, step, jnp.zeros((rows, 1), I32))


def _t5_bias_tile(dist, t5_ref, heads):
    ge = [dist >= T5_BUCKET_START[b] for b in range(1, NUM_BUCKETS)]
    out = []
    for h in heads:
        bias = jnp.full(dist.shape, t5_ref[0, h], F32)
        for b in range(1, NUM_BUCKETS):
            bias = jnp.where(ge[b - 1], t5_ref[b, h], bias)
        out.append(bias)
    return out


def _softmax_update(s, m, l):
    m_new = jnp.maximum(m, jnp.max(s, axis=-1, keepdims=True))
    m_use = jnp.where(m_new == NEG_INF, 0.0, m_new)
    alpha = jnp.exp(m - m_use)
    p = jnp.exp(s - m_use)
    return m_new, alpha * l + jnp.sum(p, axis=-1, keepdims=True), alpha, p


def _prompt_attn_kernel(t5_ref, qcat_ref, qd_ref, qi_ref, wi_ref, kcat_ref, kip0_ref, kip1_ref, kd_ref, vd_ref,
                        wv_ref, omla_ref, od_ref, sc_ref, hi_ref, acc_ref, accd_ref, band_ref, *, t, topk):
    i = pl.program_id(1)
    rows_m = MLA_HEADS * t
    row = lax.broadcasted_iota(I32, (t, t), 0)
    col = lax.broadcasted_iota(I32, (t, t), 1)
    vis = col <= row

    @pl.when((pl.program_id(0) == 0) & (i == 0))
    def _():
        left = _t5_bias_tile(row - col + t, t5_ref, range(DSA_HEADS))
        diag = _t5_bias_tile(row - col, t5_ref, range(DSA_HEADS))
        for h in range(DSA_HEADS):
            band_ref[h, :, :t] = left[h]
            band_ref[h, :, t:] = diag[h]

    qall = qcat_ref[...].reshape(rows_m, 384)
    qi_b = qi_ref[...].astype(BF16)
    wsc = wi_ref[...] * IDX_W_SCALE
    acc_ref[...] = jnp.zeros_like(acc_ref)

    def kv_step(j, carry, masked):
        m, l = carry
        start = pl.multiple_of(j * t, t)
        kc = kcat_ref[pl.ds(start, t), :]
        rg = rows_m // MLA_GROUPS
        hpg = IDX_HEADS // MLA_GROUPS
        kp = (kip0_ref[pl.ds(start, t), :], kip1_ref[pl.ds(start, t), :])
        sc = jnp.zeros((t, t), F32)
        m_new, l_new = [], []
        for g in range(MLA_GROUPS):
            rows = slice(g * rg, (g + 1) * rg)
            s = _dot_nt(qall[rows], kc)
            for h in range(g * hpg, (g + 1) * hpg):
                d = _dot_nt(qi_b[:, 128 * (h // 2):128 * (h // 2 + 1)], kp[h % 2])
                sc = sc + jnp.maximum(d, 0.0) * wsc[:, h:h + 1]
            if masked:
                s = jnp.where(vis[None], s.reshape(rg // t, t, t), NEG_INF).reshape(rg, t)
            m_g = jnp.maximum(m[g], jnp.max(s, axis=-1, keepdims=True))
            alpha = jnp.exp2((m[g] - m_g) * MLA_EXP2)
            p = jnp.exp2((s - m_g) * MLA_EXP2)
            l_new.append(alpha * l[g] + jnp.sum(p, axis=-1, keepdims=True))
            m_new.append(m_g)
            acc_ref[rows, :] = acc_ref[rows, :] * alpha + _dot(p.astype(BF16), kc[:, :KV_LORA])
        m_new, l_new = tuple(m_new), tuple(l_new)
        if masked:
            sc = jnp.where(vis, sc, NEG_INF)
        sc_ref[j] = sc
        hi_ref[j] = _top16(sc).astype(BF16)
        return m_new, l_new

    rows_g = rows_m // MLA_GROUPS
    init = (tuple(jnp.full((rows_g, 1), NEG_INF, F32) for _ in range(MLA_GROUPS)),
            tuple(jnp.zeros((rows_g, 1), F32) for _ in range(MLA_GROUPS)))
    carry = lax.fori_loop(0, i, functools.partial(kv_step, masked=False), init)
    m_mla, l_mla = kv_step(i, carry, masked=True)

    lanes = t // 128
    col_t = lax.broadcasted_iota(I32, (t, t), 1)

    def count_tiles(pred, dtype):
        def body(j, acc):
            hit = jnp.where(pred(j), jnp.ones((), dtype), jnp.zeros((), dtype))
            for u in range(lanes):
                acc = acc + hit[:, u * 128:(u + 1) * 128]
            return acc
        return _count_lanes(lax.fori_loop(0, i + 1, body, jnp.zeros((t, 128), dtype)))

    def count_hi(c):
        cb = c.astype(BF16)
        return count_tiles(lambda j: hi_ref[j] >= cb, BF16)

    thr, done = _topk_threshold(count_hi, lambda c: count_tiles(lambda j: sc_ref[j] >= c, I32), topk, t)
    thr = jnp.maximum(thr, -F32_MAX)

    @pl.when(jnp.min(done) == 0)
    def _():
        need = topk - count_tiles(lambda j: sc_ref[j] > thr, I32)
        cut = _tie_cutoff(lambda p: count_tiles(lambda j: (sc_ref[j] == thr) & (j * t + col_t < p), I32),
                          need, max(1, (sc_ref.shape[0] * t - 1).bit_length()), t)

        def drop(j, _):
            x = sc_ref[j]
            sc_ref[j] = jnp.where((x == thr) & (j * t + col_t > cut), NEG_INF, x)
            return 0
        lax.fori_loop(0, i + 1, drop, 0)

    qd_b = qd_ref[...].astype(BF16)
    qg = [jnp.concatenate([qd_b[:, (2 * g) * 128:(2 * g + 1) * 128],
                           qd_b[:, (2 * g + 1) * 128:(2 * g + 2) * 128]], axis=0) for g in range(DSA_KV_HEADS)]
    accd_ref[...] = jnp.zeros_like(accd_ref)

    def dsa_step(j, carry, kind):
        start = pl.multiple_of(j * t, t)
        sel = sc_ref[j] >= thr
        new = []
        for g in range(DSA_KV_HEADS):
            m, l = carry[g]
            kd = kd_ref[pl.ds(start, t), g * 128:(g + 1) * 128]
            vd = vd_ref[pl.ds(start, t), g * 128:(g + 1) * 128]
            s = _dot_nt(qg[g], kd) * DSA_SCALE
            parts = []
            for r in range(2):
                h = 2 * g + r
                if kind == "far":
                    b = t5_ref[NUM_BUCKETS - 1, h]
                elif kind == "left":
                    b = band_ref[h, :, :t]
                else:
                    b = band_ref[h, :, t:]
                parts.append(jnp.where(sel, s[r * t:(r + 1) * t] + b, NEG_INF))
            s = jnp.concatenate(parts, axis=0)
            m_new, l_new, alpha, p = _softmax_update(s, m, l)
            accd_ref[g] = accd_ref[g] * alpha + _dot(p.astype(BF16), vd)
            new.append((m_new, l_new))
        return tuple(new)

    init_d = tuple((jnp.full((2 * t, 1), NEG_INF, F32), jnp.zeros((2 * t, 1), F32)) for _ in range(DSA_KV_HEADS))
    carry_d = lax.fori_loop(0, jnp.maximum(i - 1, 0), functools.partial(dsa_step, kind="far"), init_d)
    carry_d = lax.fori_loop(jnp.maximum(i - 1, 0), i, functools.partial(dsa_step, kind="left"), carry_d)
    carry_d = dsa_step(i, carry_d, "diag")

    for g in range(DSA_KV_HEADS):
        o = accd_ref[g] / carry_d[g][1]
        for r in range(2):
            h = 2 * g + r
            od_ref[:, h * 128:(h + 1) * 128] = o[r * t:(r + 1) * t]
    heads_g = MLA_HEADS // MLA_GROUPS
    for g in range(MLA_GROUPS):
        lat = (acc_ref[g * rows_g:(g + 1) * rows_g, :] / l_mla[g]).astype(BF16)
        for hh in range(heads_g):
            h = g * heads_g + hh
            omla_ref[:, h * 128:(h + 1) * 128] = _dot(lat[hh * t:(hh + 1) * t], wv_ref[h])


def _prompt_attn(t5, qcat8, z, kcat, kip0, kip1, kdb, vdb, wv, *, batch, seq, t, topk):
    assert t >= T5_FAR and seq % t == 0 and seq // 128 <= 256
    n = batch * seq
    nq = seq // t
    qrow = lambda c, w: pl.BlockSpec((t, w), lambda b, i, c=c, w=w: (b * nq + i, c // w))
    kfull = lambda w: pl.BlockSpec((seq, w), lambda b, i: (b, 0))
    kern = functools.partial(_prompt_attn_kernel, t=t, topk=topk)
    return pl.pallas_call(
        kern,
        out_shape=[jax.ShapeDtypeStruct((n, MLA_HEADS * HEAD_DIM), F32),
                   jax.ShapeDtypeStruct((n, DSA_HEADS * HEAD_DIM), F32)],
        grid=(batch, nq),
        in_specs=[pl.BlockSpec(memory_space=pltpu.SMEM),
                  pl.BlockSpec((MLA_HEADS, t, 384), lambda b, i: (0, b * nq + i, 0)),
                  qrow(C_QD, 512), qrow(C_QI, 1024), qrow(C_WI, 128),
                  kfull(384), kfull(128), kfull(128), kfull(256), kfull(256),
                  pl.BlockSpec(wv.shape, lambda b, i: (0, 0, 0))],
        out_specs=[pl.BlockSpec((t, MLA_HEADS * HEAD_DIM), lambda b, i: (b * nq + i, 0)),
                   pl.BlockSpec((t, DSA_HEADS * HEAD_DIM), lambda b, i: (b * nq + i, 0))],
        scratch_shapes=[pltpu.VMEM((nq, t, t), F32),
                        pltpu.VMEM((nq, t, t), BF16),
                        pltpu.VMEM((MLA_HEADS * t, KV_LORA), F32),
                        pltpu.VMEM((DSA_KV_HEADS, 2 * t, HEAD_DIM), F32),
                        pltpu.VMEM((DSA_HEADS, t, 2 * t), F32)],
        compiler_params=_cparams(("arbitrary", "arbitrary")),
        name="prompt_attn",
    )(t5, qcat8, z, z, z, kcat, kip0, kip1, kdb, vdb, wv)


def _pad_rows(x, rows):
    return jnp.concatenate([x, jnp.zeros((rows - x.shape[0], x.shape[1]), x.dtype)], axis=0)


N_SLOTS = 4


def _chunk_stream(n, nb, nch, chunk_copies):
    total = nb * nch

    def run(g, action):
        gc = jnp.minimum(g, total - 1)
        for cp in chunk_copies(gc // nch, gc % nch, g % N_SLOTS):
            getattr(cp, action)()

    def prologue():
        @pl.when(n == 0)
        def _():
            for g in range(N_SLOTS - 1):
                run(g, "start")

    def advance(c):
        g = n * nch + c
        run(g + N_SLOTS - 1, "start")
        run(g, "wait")
        return g % N_SLOTS

    def epilogue():
        @pl.when(n == nb - 1)
        def _():
            for g in range(total, total + N_SLOTS - 1):
                run(g, "wait")

    return prologue, advance, epilogue


def _sample_mla_kernel(pt_ref, qcat_ref, qi_ref, we_ref, wo_ref, ckvn_ref, krn_ref, kin_ref, wv_ref,
                       ckv_hbm, krt_hbm, kit_hbm, sc_ref, omla_ref, ckv_buf, krt_buf, kit_buf, sem, *, c_pages, nb):
    n = pl.program_id(0)
    nch = pt_ref.shape[1] // c_pages
    cw = c_pages * PAGE_SIZE
    past = nch * cw
    t_new = ckvn_ref.shape[0]
    rows_m = MLA_HEADS * t_new

    def chunk_copies(seq, c, slot):
        cps = []
        for u in range(c_pages):
            page = pt_ref[seq, c * c_pages + u]
            rows = pl.ds(u * PAGE_SIZE, PAGE_SIZE)
            cps.append(pltpu.make_async_copy(ckv_hbm.at[0, page], ckv_buf.at[slot, rows, :], sem.at[0, slot]))
            cps.append(pltpu.make_async_copy(krt_hbm.at[0, page], krt_buf.at[slot, :, rows], sem.at[1, slot]))
            cps.append(pltpu.make_async_copy(kit_hbm.at[0, page], kit_buf.at[slot, :, rows], sem.at[2, slot]))
        return cps

    prologue, advance, epilogue = _chunk_stream(n, nb, nch, chunk_copies)
    prologue()

    q = qcat_ref[...].reshape(rows_m, 384).astype(BF16)
    ql = q[:, :KV_LORA]
    qr = q[:, KV_LORA:KV_LORA + QK_ROPE]
    rows_i = qi_ref.shape[1]
    q_idx = qi_ref[...].reshape(2 * rows_i, IDX_DIM).astype(BF16)
    w_idx = jnp.concatenate([we_ref[...], wo_ref[...]], axis=0) * IDX_W_SCALE

    def idx_scores(dots):
        d = jnp.maximum(dots, 0.0) * w_idx
        return jnp.sum(d.reshape(IDX_HEADS, t_new, d.shape[-1]), axis=0)

    def chunk(c, carry):
        m, l, acc = carry
        slot = advance(c)
        ck = ckv_buf[slot].astype(BF16)
        s = (_dot_nt(ql, ck) + _dot(qr, krt_buf[slot].astype(BF16))) * MLA_SCALE
        m, l, alpha, p = _softmax_update(s, m, l)
        acc = acc * alpha + _dot(p.astype(BF16), ck)
        kit = kit_buf[slot].astype(BF16)
        sc_ref[:, pl.ds(pl.multiple_of(c * cw, cw), cw)] = idx_scores(_dot(q_idx, kit))
        return m, l, acc

    init = (jnp.full((rows_m, 1), NEG_INF, F32), jnp.zeros((rows_m, 1), F32), jnp.zeros((rows_m, KV_LORA), F32))
    m, l, acc = lax.fori_loop(0, nch, chunk, init)
    epilogue()

    tok = lax.broadcasted_iota(I32, (t_new, PAGE_SIZE), 0)
    col = lax.broadcasted_iota(I32, (t_new, PAGE_SIZE), 1)
    vis = col <= tok
    ck = _pad_rows(ckvn_ref[...], PAGE_SIZE).astype(BF16)
    kr = _pad_rows(krn_ref[...], PAGE_SIZE).astype(BF16)
    ki = _pad_rows(kin_ref[...], PAGE_SIZE).astype(BF16)
    s = (_dot_nt(ql, ck) + _dot_nt(qr, kr)) * MLA_SCALE
    s = jnp.where(vis[None], s.reshape(MLA_HEADS, t_new, PAGE_SIZE), NEG_INF).reshape(rows_m, PAGE_SIZE)
    m, l, alpha, p = _softmax_update(s, m, l)
    acc = acc * alpha + _dot(p.astype(BF16), ck)
    sc_ref[:, past:] = jnp.where(vis, idx_scores(_dot_nt(q_idx, ki)), NEG_INF)
    lat = (acc / l).astype(BF16)
    for h in range(MLA_HEADS):
        omla_ref[:, h * 128:(h + 1) * 128] = _dot(lat[h * t_new:(h + 1) * t_new], wv_ref[h])


def _sample_mla(page_table, qcat8, qi_eo, w_even, w_odd, ckv_new, kr_new, ki_new, wv, ckv_pool, krt_pool, kit_pool,
                *, nb, t_new, c_pages):
    npages = page_table.shape[1]
    nch = npages // c_pages
    assert nch * c_pages == npages
    cw = c_pages * PAGE_SIZE
    width = npages * PAGE_SIZE + PAGE_SIZE
    rows_i = (IDX_HEADS // 2) * t_new
    any_spec = pl.BlockSpec(memory_space=pl.ANY)
    in_specs = [pl.BlockSpec((MLA_HEADS, t_new, 384), lambda n, pt: (0, n, 0)),
                pl.BlockSpec((None, 2, rows_i, IDX_DIM), lambda n, pt: (n, 0, 0, 0)),
                pl.BlockSpec((None, rows_i, 1), lambda n, pt: (n, 0, 0)),
                pl.BlockSpec((None, rows_i, 1), lambda n, pt: (n, 0, 0)),
                pl.BlockSpec((t_new, KV_LORA), lambda n, pt: (n, 0)),
                pl.BlockSpec((t_new, QK_ROPE), lambda n, pt: (n, 0)),
                pl.BlockSpec((t_new, IDX_DIM), lambda n, pt: (n, 0)),
                pl.BlockSpec(wv.shape, lambda n, pt: (0, 0, 0)),
                any_spec, any_spec, any_spec]
    grid_spec = pltpu.PrefetchScalarGridSpec(
        num_scalar_prefetch=1, grid=(nb,), in_specs=in_specs,
        out_specs=[pl.BlockSpec((None, t_new, width), lambda n, pt: (n, 0, 0)),
                   pl.BlockSpec((t_new, MLA_HEADS * HEAD_DIM), lambda n, pt: (n, 0))],
        scratch_shapes=[pltpu.VMEM((N_SLOTS, cw, KV_LORA), F32), pltpu.VMEM((N_SLOTS, QK_ROPE, cw), F32),
                        pltpu.VMEM((N_SLOTS, IDX_DIM, cw), F32), pltpu.SemaphoreType.DMA((3, N_SLOTS))])
    return pl.pallas_call(
        functools.partial(_sample_mla_kernel, c_pages=c_pages, nb=nb),
        out_shape=[jax.ShapeDtypeStruct((nb, t_new, width), F32),
                   jax.ShapeDtypeStruct((nb * t_new, MLA_HEADS * HEAD_DIM), F32)],
        grid_spec=grid_spec,
        compiler_params=_cparams(("arbitrary",)),
        name="sample_mla",
    )(page_table, qcat8, qi_eo, w_even, w_odd, ckv_new, kr_new, ki_new, wv, ckv_pool, krt_pool, kit_pool)


def _sample_thr_kernel(sc_ref, thr_ref, cut_ref, hi_ref, *, topk, chunk):
    nseq, t_new, width = sc_ref.shape
    rows = nseq * t_new
    lanes = chunk // 128
    col_c = lax.broadcasted_iota(I32, (rows, chunk), 1)

    def load(c_idx):
        return sc_ref[:, :, pl.ds(pl.multiple_of(c_idx * chunk, chunk), chunk)].reshape(rows, chunk)

    def fill(c_idx, _):
        hi_ref[:, pl.ds(pl.multiple_of(c_idx * chunk, chunk), chunk)] = _top16(load(c_idx)).astype(BF16)
        return 0
    lax.fori_loop(0, width // chunk, fill, 0)

    def count_chunks(pred, dtype):
        def body(c_idx, acc):
            hit = jnp.where(pred(c_idx), jnp.ones((), dtype), jnp.zeros((), dtype))
            for u in range(lanes):
                acc = acc + hit[:, u * 128:(u + 1) * 128]
            return acc
        return _count_lanes(lax.fori_loop(0, width // chunk, body, jnp.zeros((rows, 128), dtype)))

    def count_hi(c):
        cb = c.astype(BF16)
        return count_chunks(lambda ci: hi_ref[:, pl.ds(pl.multiple_of(ci * chunk, chunk), chunk)] >= cb, BF16)

    thr, done = _topk_threshold(count_hi, lambda c: count_chunks(lambda ci: load(ci) >= c, I32), topk, rows)
    thr = jnp.maximum(thr, -F32_MAX)
    thr_ref[...] = thr
    cut_ref[...] = jnp.full((rows, 1), np.iinfo(np.int32).max, I32)

    @pl.when(jnp.min(done) == 0)
    def _():
        need = topk - count_chunks(lambda ci: load(ci) > thr, I32)
        cut_ref[...] = _tie_cutoff(
            lambda p: count_chunks(lambda ci: (load(ci) == thr) & (ci * chunk + col_c < p), I32),
            need, max(1, (width - 1).bit_length()), rows)


def _sample_thr(scores, *, topk, nseq=16):
    nb, t_new, width = scores.shape
    nseq = min(nseq, nb)
    lanes = width // 128
    assert lanes <= 256
    chunk = 128 * max(d for d in range(1, 5) if lanes % d == 0)
    rows = nseq * t_new
    return pl.pallas_call(
        functools.partial(_sample_thr_kernel, topk=topk, chunk=chunk),
        out_shape=[jax.ShapeDtypeStruct((nb * t_new, 1), F32), jax.ShapeDtypeStruct((nb * t_new, 1), I32)],
        grid=(nb // nseq,),
        in_specs=[pl.BlockSpec((nseq, t_new, width), lambda i: (i, 0, 0))],
        out_specs=[pl.BlockSpec((rows, 1), lambda i: (i, 0)), pl.BlockSpec((rows, 1), lambda i: (i, 0))],
        scratch_shapes=[pltpu.VMEM((rows, width), BF16)],
        compiler_params=_cparams(("parallel",)),
        name="sample_thr",
    )(scores)


def _sample_dsa_kernel(pt_ref, t5_ref, qd_ref, sc_ref, thr_ref, cut_ref, kdn_ref, vdn_ref, kd_hbm, vd_hbm, od_ref,
                       kd_buf, vd_buf, sem, *, c_pages, nb):
    n = pl.program_id(0)
    nch = pt_ref.shape[1] // c_pages
    cw = c_pages * PAGE_SIZE
    past = nch * cw
    t_new = qd_ref.shape[0]

    def chunk_copies(seq, c, slot):
        cps = []
        for u in range(c_pages):
            page = pt_ref[seq, c * c_pages + u]
            rows = pl.ds(u * PAGE_SIZE, PAGE_SIZE)
            for g in range(DSA_KV_HEADS):
                cps.append(pltpu.make_async_copy(kd_hbm.at[0, page, :, g, :], kd_buf.at[slot, g, rows, :], sem.at[0, slot]))
                cps.append(pltpu.make_async_copy(vd_hbm.at[0, page, :, g, :], vd_buf.at[slot, g, rows, :], sem.at[1, slot]))
        return cps

    prologue, advance, epilogue = _chunk_stream(n, nb, nch, chunk_copies)
    prologue()

    qd_b = qd_ref[...].astype(BF16)
    qg = [jnp.concatenate([qd_b[:, (2 * g) * 128:(2 * g + 1) * 128],
                           qd_b[:, (2 * g + 1) * 128:(2 * g + 2) * 128]], axis=0) for g in range(DSA_KV_HEADS)]
    thr = thr_ref[...]
    cut = cut_ref[...]

    def selected(start, width):
        x = sc_ref[:, pl.ds(start, width)]
        col = start + lax.broadcasted_iota(I32, (t_new, width), 1)
        return (x > thr) | ((x == thr) & (col <= cut))

    def attend(carry, k_of, v_of, sel, dist):
        if dist is not None:
            bias = _t5_bias_tile(dist, t5_ref, range(DSA_HEADS))
        new = []
        for g in range(DSA_KV_HEADS):
            m, l, acc = carry[g]
            s = _dot_nt(qg[g], k_of(g)) * DSA_SCALE
            parts = []
            for r in range(2):
                h = 2 * g + r
                b = bias[h] if dist is not None else t5_ref[NUM_BUCKETS - 1, h]
                parts.append(jnp.where(sel, s[r * t_new:(r + 1) * t_new] + b, NEG_INF))
            m_new, l_new, alpha, p = _softmax_update(jnp.concatenate(parts, axis=0), m, l)
            new.append((m_new, l_new, acc * alpha + _dot(p.astype(BF16), v_of(g))))
        return tuple(new)

    def chunk(c, carry, near):
        slot = advance(c)
        start = pl.multiple_of(c * cw, cw)
        dist = None
        if near:
            tok = lax.broadcasted_iota(I32, (t_new, cw), 0)
            col = lax.broadcasted_iota(I32, (t_new, cw), 1)
            dist = (past + tok) - (start + col)
        return attend(carry, lambda g: kd_buf[slot, g].astype(BF16), lambda g: vd_buf[slot, g].astype(BF16),
                      selected(start, cw), dist)

    init = tuple((jnp.full((2 * t_new, 1), NEG_INF, F32), jnp.zeros((2 * t_new, 1), F32),
                  jnp.zeros((2 * t_new, HEAD_DIM), F32)) for _ in range(DSA_KV_HEADS))
    carry = lax.fori_loop(0, nch - 1, functools.partial(chunk, near=False), init)
    carry = chunk(nch - 1, carry, near=True)
    epilogue()

    tok = lax.broadcasted_iota(I32, (t_new, PAGE_SIZE), 0)
    col = lax.broadcasted_iota(I32, (t_new, PAGE_SIZE), 1)
    kdn = _pad_rows(kdn_ref[...], PAGE_SIZE).astype(BF16)
    vdn = _pad_rows(vdn_ref[...], PAGE_SIZE).astype(BF16)
    carry = attend(carry, lambda g: kdn[:, g * 128:(g + 1) * 128], lambda g: vdn[:, g * 128:(g + 1) * 128],
                   selected(past, PAGE_SIZE), tok - col)
    for g in range(DSA_KV_HEADS):
        o = carry[g][2] / carry[g][1]
        for r in range(2):
            h = 2 * g + r
            od_ref[:, h * 128:(h + 1) * 128] = o[r * t_new:(r + 1) * t_new]


def _sample_dsa(page_table, t5, z, scores, thr, cut, kd_new, vd_new, kd_pool, vd_pool, *, nb, t_new, c_pages):
    npages = page_table.shape[1]
    nch = npages // c_pages
    assert nch * c_pages == npages and c_pages * PAGE_SIZE >= T5_FAR
    cw = c_pages * PAGE_SIZE
    width = scores.shape[-1]
    any_spec = pl.BlockSpec(memory_space=pl.ANY)
    in_specs = [pl.BlockSpec(memory_space=pltpu.SMEM),
                pl.BlockSpec((t_new, 512), lambda n, pt: (n, C_QD // 512)),
                pl.BlockSpec((None, t_new, width), lambda n, pt: (n, 0, 0)),
                pl.BlockSpec((t_new, 1), lambda n, pt: (n, 0)),
                pl.BlockSpec((t_new, 1), lambda n, pt: (n, 0)),
                pl.BlockSpec((t_new, 256), lambda n, pt: (n, 0)),
                pl.BlockSpec((t_new, 256), lambda n, pt: (n, 0)),
                any_spec, any_spec]
    grid_spec = pltpu.PrefetchScalarGridSpec(
        num_scalar_prefetch=1, grid=(nb,), in_specs=in_specs,
        out_specs=pl.BlockSpec((t_new, DSA_HEADS * HEAD_DIM), lambda n, pt: (n, 0)),
        scratch_shapes=[pltpu.VMEM((N_SLOTS, DSA_KV_HEADS, cw, HEAD_DIM), F32),
                        pltpu.VMEM((N_SLOTS, DSA_KV_HEADS, cw, HEAD_DIM), F32),
                        pltpu.SemaphoreType.DMA((2, N_SLOTS))])
    return pl.pallas_call(
        functools.partial(_sample_dsa_kernel, c_pages=c_pages, nb=nb),
        out_shape=jax.ShapeDtypeStruct((nb * t_new, DSA_HEADS * HEAD_DIM), F32),
        grid_spec=grid_spec,
        compiler_params=_cparams(("arbitrary",)),
        name="sample_dsa",
    )(page_table, t5, z, scores, thr, cut, kd_new, vd_new, kd_pool, vd_pool)


def _mem_attn_kernel(q_ref, k_ref, v_ref, o_ref):
    for s in range(q_ref.shape[0]):
        q = q_ref[s].astype(BF16)
        for h in range(MEM_HEADS):
            k = k_ref[s, :, h, :].astype(BF16)
            v = v_ref[s, :, h, :].astype(BF16)
            logits = _dot_nt(q[:, h * 128:(h + 1) * 128], k) * MEM_SCALE
            p = jnp.exp(logits - jnp.max(logits, axis=-1, keepdims=True))
            o = _dot(p.astype(BF16), v) / jnp.sum(p, axis=-1, keepdims=True)
            o_ref[s, :, h * 128:(h + 1) * 128] = o


def _mem_attn(z3, mem_k, mem_v, *, seqs_per_step, rows_per_step):
    groups, rows, _ = z3.shape
    n_mem = mem_k.shape[1]
    gs, rs = seqs_per_step, rows_per_step
    kv_spec = pl.BlockSpec((gs, n_mem, MEM_HEADS, HEAD_DIM), lambda a, b: (a, 0, 0, 0))
    return pl.pallas_call(
        _mem_attn_kernel,
        out_shape=jax.ShapeDtypeStruct((groups, rows, MEM_HEADS * HEAD_DIM), F32),
        grid=(groups // gs, rows // rs),
        in_specs=[pl.BlockSpec((gs, rs, 512), lambda a, b: (a, b, C_QM // 512)), kv_spec, kv_spec],
        out_specs=pl.BlockSpec((gs, rs, 512), lambda a, b: (a, b, 0)),
        compiler_params=_cparams(("parallel", "arbitrary")),
        name="mem_attn",
    )(z3, mem_k, mem_v)


def _mem_attn_rows_kernel(q_ref, k_ref, v_ref, o_ref):
    gs, t_new, _ = q_ref.shape
    nrow = k_ref.shape[1]
    rows = MEM_HEADS * t_new
    same_head = (lax.broadcasted_iota(I32, (rows, nrow), 0) // t_new
                 == lax.broadcasted_iota(I32, (rows, nrow), 1) % MEM_HEADS)
    for s in range(gs):
        q = q_ref[s]
        qs = jnp.concatenate([q[:, h * 128:(h + 1) * 128] for h in range(MEM_HEADS)], axis=0).astype(BF16)
        logits = jnp.where(same_head, _dot_nt(qs, k_ref[s].astype(BF16)) * MEM_SCALE, NEG_INF)
        p = jnp.exp(logits - jnp.max(logits, axis=-1, keepdims=True))
        o = _dot(p.astype(BF16), v_ref[s].astype(BF16)) / jnp.sum(p, axis=-1, keepdims=True)
        for h in range(MEM_HEADS):
            o_ref[s, :, h * 128:(h + 1) * 128] = o[h * t_new:(h + 1) * t_new]


def _mem_attn_rows(z3, mem_k, mem_v, *, seqs_per_step):
    groups, t_new, _ = z3.shape
    gs = seqs_per_step
    kv_spec = pl.BlockSpec((gs,) + mem_k.shape[1:], lambda a: (a, 0, 0))
    return pl.pallas_call(
        _mem_attn_rows_kernel,
        out_shape=jax.ShapeDtypeStruct((groups, t_new, MEM_HEADS * HEAD_DIM), F32),
        grid=(groups // gs,),
        in_specs=[pl.BlockSpec((gs, t_new, 512), lambda a: (a, 0, C_QM // 512)), kv_spec, kv_spec],
        out_specs=pl.BlockSpec((gs, t_new, 512), lambda a: (a, 0, 0)),
        compiler_params=_cparams(("parallel",)),
        name="mem_attn_rows",
    )(z3, mem_k, mem_v)


def _silu(g):
    return g * (1.0 / (1.0 + jnp.exp(-g)))


def _merge_kernel(x_ref, omla_ref, od_ref, om_ref, gmla_ref, gdsa_ref, gmem_ref, wout_ref, fg_ref, y_ref):
    o = jnp.concatenate([(omla_ref[...] * _silu(gmla_ref[...])).astype(BF16),
                         (od_ref[...] * _silu(gdsa_ref[...])).astype(BF16),
                         (om_ref[...] * _silu(gmem_ref[...])).astype(BF16)], axis=-1)
    xo = x_ref[...] + _dot(o, wout_ref[...])
    y_ref[...] = _rms(xo, fg_ref[...])


def _merge(x, omla, od, om, z, wout, fg):
    n, d = x.shape
    tm = min(256, n)
    row = lambda w, c=0: pl.BlockSpec((tm, w), lambda i, c=c, w=w: (i, c // w))
    return pl.pallas_call(
        _merge_kernel,
        out_shape=jax.ShapeDtypeStruct((n, d), F32),
        grid=(n // tm,),
        in_specs=[row(d), row(1024), row(512), row(512), row(1024, C_GMLA), row(512, C_GDSA), row(512, C_GMEM),
                  pl.BlockSpec(wout.shape, lambda i: (0, 0)), pl.BlockSpec((1, d), lambda i: (0, 0))],
        out_specs=row(d),
        compiler_params=_cparams(("parallel",)),
        name="merge",
    )(x, omla, od, om, z, z, z, wout, fg)


def _rope_tables(pos):
    half = QK_ROPE // 2
    inv_freq = jnp.exp(-math.log(ROPE_THETA) * jnp.arange(half, dtype=F32) / half)
    ang = pos.astype(F32)[:, None] * inv_freq[None, :]
    cos, sin = jnp.cos(ang), jnp.sin(ang)
    return jnp.tile(cos, (1, 4)), jnp.concatenate([-sin, sin, -sin, sin], axis=1)


def _permute_w_in(w):
    d = w.shape[0]
    seg = lambda a, b: w[:, a:b]
    return jnp.concatenate([
        seg(0, 512), seg(512, 768), seg(768, 832), seg(3904, 3968), seg(3968, 3984), jnp.zeros((d, 112), w.dtype),
        seg(832, 1856), seg(1856, 2368), seg(2368, 2624), seg(2624, 2880), seg(2880, 3904), seg(3984, 4496),
        seg(4496, 5008), seg(5008, 5520)], axis=1).astype(BF16)


def kernel(x_prompt, x_sample, mem_prompt, cache_mla_ckv, cache_mla_krope, cache_dsa_k, cache_dsa_v, cache_idx_k,
           cache_mem_k, cache_mem_v, page_table, t5_bias, norm_g, w_in, qa_norm_g, w_uq, kva_norm_g, w_ukv,
           idx_norm_g, mem_norm_g, w_mem_kv, w_out, final_norm_g, *, t_blk=256, c_pages=16):
    bp, seq, d = x_prompt.shape
    nb, t_new, _ = x_sample.shape
    n_mem = mem_prompt.shape[1]
    npages = page_table.shape[1]
    past = npages * PAGE_SIZE
    depth = norm_g.shape[0]
    assert depth == 1
    l = 0
    t_blk = min(t_blk, seq)
    c_pages = min(c_pages, npages // 2)
    topk_p = min(TOPK_MAX, seq // 4)
    topk_s = min(TOPK_MAX, (past + t_new) // 4)

    w_in_p = _permute_w_in(w_in[l])
    wuq = jnp.concatenate([w_uq[l][:, :, :HEAD_DIM].reshape(Q_LORA, -1),
                           w_uq[l][:, :, HEAD_DIM:].reshape(Q_LORA, -1)], axis=1).astype(BF16)
    wk = jnp.transpose(w_ukv[l][:, :, :HEAD_DIM], (1, 2, 0)).astype(BF16)
    wv = jnp.transpose(w_ukv[l][:, :, HEAD_DIM:], (1, 0, 2)).astype(BF16)
    wout = w_out[l].astype(BF16)
    wmem = w_mem_kv[l].astype(BF16)
    g_row = lambda g: g.reshape(1, -1)
    idxg128 = jnp.concatenate([jnp.zeros((1, 64), F32), g_row(idx_norm_g[l])], axis=1)
    fg = g_row(final_norm_g)

    def branch(x2, cos_t, sin_t, q_dtype):
        z = _in_proj(x2, g_row(norm_g[l]), w_in_p)
        outs = _post(z, cos_t, sin_t, g_row(qa_norm_g[l]), g_row(kva_norm_g[l]), idxg128, wuq, wk, q_dtype)
        return z, outs

    xp2 = x_prompt.reshape(bp * seq, d)
    cos_p, sin_p = _rope_tables(jnp.arange(seq, dtype=I32))
    zp, (ckv_p, krope_p, ki_p, kd_p, vd_p, kcat_p, kip0_p, kip1_p, kdb_p, vdb_p, qcat_p) = branch(xp2, cos_p, sin_p, BF16)
    omla_p, od_p = _prompt_attn(t5_bias, qcat_p, zp, kcat_p, kip0_p, kip1_p, kdb_p, vdb_p, wv,
                                batch=bp, seq=seq, t=t_blk, topk=topk_p)
    zmem = _in_proj(mem_prompt.reshape(bp * n_mem, d), g_row(mem_norm_g[l]), wmem)
    mk_p = zmem[:, :MEM_HEADS * HEAD_DIM].reshape(bp, n_mem, MEM_HEADS, HEAD_DIM)
    mv_p = zmem[:, MEM_HEADS * HEAD_DIM:].reshape(bp, n_mem, MEM_HEADS, HEAD_DIM)
    om_p = _mem_attn(zp.reshape(bp, seq, D_Z), mk_p, mv_p, seqs_per_step=1, rows_per_step=min(512, seq))
    y_prompt = _merge(xp2, omla_p, od_p, om_p.reshape(bp * seq, -1), zp, wout, fg).reshape(bp, seq, d)

    xs2 = x_sample.reshape(nb * t_new, d)
    tm_s = min(256, nb * t_new)
    pos_s = past + (jnp.arange(tm_s, dtype=I32) % t_new)
    cos_s, sin_s = _rope_tables(pos_s)
    zs, (ckv_s, krope_s, ki_s, kd_s, vd_s, _, _, _, _, _, qcat_s) = branch(xs2, cos_s, sin_s, F32)
    wi_s = zs[:, C_WI:C_WI + IDX_HEADS].reshape(nb, t_new, IDX_HEADS // 2, 2)
    rows_i = (IDX_HEADS // 2) * t_new
    w_even = jnp.transpose(wi_s[..., 0], (0, 2, 1)).reshape(nb, rows_i, 1)
    w_odd = jnp.transpose(wi_s[..., 1], (0, 2, 1)).reshape(nb, rows_i, 1)
    qi_eo = jnp.transpose(zs[:, C_QI:C_QI + IDX_HEADS * IDX_DIM].reshape(nb, t_new, IDX_HEADS // 2, 2, IDX_DIM),
                          (0, 3, 2, 1, 4)).reshape(nb, 2, rows_i, IDX_DIM)
    krt_pool = jnp.swapaxes(cache_mla_krope, 2, 3)
    kit_pool = jnp.swapaxes(cache_idx_k, 2, 3)
    scores, omla_s = _sample_mla(page_table, qcat_s, qi_eo, w_even, w_odd, ckv_s, krope_s, ki_s, wv,
                                 cache_mla_ckv, krt_pool, kit_pool, nb=nb, t_new=t_new, c_pages=c_pages)
    thr, cut = _sample_thr(scores, topk=topk_s)
    od_s = _sample_dsa(page_table, t5_bias, zs, scores, thr, cut, kd_s, vd_s, cache_dsa_k, cache_dsa_v,
                       nb=nb, t_new=t_new, c_pages=c_pages)
    mem_rows = lambda a: a.reshape(nb, a.shape[1] * MEM_HEADS, HEAD_DIM)
    om_s = _mem_attn_rows(zs.reshape(nb, t_new, D_Z), mem_rows(cache_mem_k[l]), mem_rows(cache_mem_v[l]),
                          seqs_per_step=min(8, nb))
    y_sample = _merge(xs2, omla_s, od_s, om_s.reshape(nb * t_new, -1), zs, wout, fg).reshape(nb, t_new, d)

    st = lambda a, *shape: a.reshape((1,) + shape)
    return (y_prompt, y_sample,
            st(ckv_p, bp, seq, KV_LORA), st(krope_p, bp, seq, QK_ROPE),
            st(kd_p, bp, seq, DSA_KV_HEADS, HEAD_DIM), st(vd_p, bp, seq, DSA_KV_HEADS, HEAD_DIM),
            st(ki_p, bp, seq, IDX_DIM), st(mk_p, bp, n_mem, MEM_HEADS, HEAD_DIM), st(mv_p, bp, n_mem, MEM_HEADS, HEAD_DIM),
            st(ckv_s, nb, t_new, KV_LORA), st(krope_s, nb, t_new, QK_ROPE),
            st(kd_s, nb, t_new, DSA_KV_HEADS, HEAD_DIM), st(vd_s, nb, t_new, DSA_KV_HEADS, HEAD_DIM),
            st(ki_s, nb, t_new, IDX_DIM))
```

```python
import functools
import math

import numpy as np
import jax
import jax.numpy as jnp
from jax import lax
from jax.experimental import pallas as pl
from jax.experimental.pallas import tpu as pltpu

F32 = jnp.float32
BF16 = jnp.bfloat16
I32 = jnp.int32

HEAD_DIM = 128
MLA_HEADS = 8
DSA_HEADS = 4
DSA_KV_HEADS = 2
MEM_HEADS = 4
Q_LORA = 512
KV_LORA = 256
QK_ROPE = 64
IDX_HEADS = 16
IDX_DIM = 64
TOPK_MAX = 256
PAGE_SIZE = 128
NUM_BUCKETS = 32
ROPE_THETA = 10000.0
MLA_SCALE = (HEAD_DIM + QK_ROPE) ** -0.5
MLA_EXP2 = MLA_SCALE * math.log2(math.e)
MLA_GROUPS = 4
DSA_SCALE = HEAD_DIM ** -0.5
MEM_SCALE = HEAD_DIM ** -0.5
IDX_W_SCALE = (IDX_DIM ** -0.5) * (IDX_HEADS ** -0.5)
EPS = 1e-6
NEG_INF = float("-inf")
F32_MAX = float(np.finfo(np.float32).max)

C_Q, C_KV, C_KRKI, C_WI, C_GMLA, C_QD, C_KD, C_VD, C_QI, C_GDSA, C_QM, C_GMEM, D_Z = (
    0, 512, 768, 896, 1024, 2048, 2560, 2816, 3072, 4096, 4608, 5120, 5632)

T5_BUCKET_START = tuple(range(17)) + (19, 21, 24, 27, 31, 35, 40, 46, 52, 59, 67, 77, 87, 99, 113)
T5_FAR = T5_BUCKET_START[-1]

VMEM_LIMIT = 56 * 1024 * 1024


def _cparams(sem):
    return pltpu.CompilerParams(dimension_semantics=sem, vmem_limit_bytes=VMEM_LIMIT)


def _dot_nt(a, b):
    return lax.dot_general(a, b, (((1,), (1,)), ((), ())), preferred_element_type=F32)


def _dot(a, b):
    return jnp.dot(a, b, preferred_element_type=F32)


def _rms(x, g):
    return x * lax.rsqrt(jnp.mean(x * x, axis=-1, keepdims=True) + EPS) * g


def _inproj_kernel(x_ref, g_ref, w_ref, z_ref, h_ref):
    @pl.when(pl.program_id(1) == 0)
    def _():
        h_ref[...] = _rms(x_ref[...], g_ref[...]).astype(BF16)

    z_ref[...] = _dot(h_ref[...], w_ref[...])


def _in_proj(x, g, w, tn=512):
    n, d = x.shape
    m = w.shape[1]
    tm = min(1024, n)
    return pl.pallas_call(
        _inproj_kernel,
        out_shape=jax.ShapeDtypeStruct((n, m), F32),
        grid=(n // tm, m // tn),
        in_specs=[pl.BlockSpec((tm, d), lambda i, j: (i, 0)),
                  pl.BlockSpec((1, d), lambda i, j: (0, 0)),
                  pl.BlockSpec((d, tn), lambda i, j: (0, j))],
        out_specs=pl.BlockSpec((tm, tn), lambda i, j: (i, j)),
        scratch_shapes=[pltpu.VMEM((tm, d), BF16)],
        compiler_params=_cparams(("parallel", "arbitrary")),
        name="in_proj",
    )(x, g, w)


def _post_kernel(cq_ref, ckv_ref, krki_ref, kd_ref, vd_ref, cos_ref, sin_ref, qag_ref, kvag_ref, idxg_ref,
                 wuq_ref, wk_ref,
                 ckv_o, krope_o, ki_o, kd_o, vd_o, kcat_o, kip0_o, kip1_o, kdb_o, vdb_o, qcat_o):
    tm = cq_ref.shape[0]
    lane = lax.broadcasted_iota(I32, (tm, 128), 1)
    hi = lane >= 64
    second_half = (lane % 64) >= 32
    cos_t = cos_ref[...]
    sin_t = sin_ref[...]

    def rope128(x):
        swapped = jnp.where(second_half, pltpu.roll(x, 32, 1), pltpu.roll(x, 96, 1))
        return x * cos_t + swapped * sin_t

    ckv_n = _rms(ckv_ref[...], kvag_ref[...])
    ckv_o[...] = ckv_n

    krki = krki_ref[...]
    rot = rope128(krki)
    krope_o[...] = rot[:, :QK_ROPE]
    ms = jnp.sum(jnp.where(hi, krki * krki, 0.0), axis=-1, keepdims=True) * (1.0 / IDX_DIM)
    ki_hi = krki * lax.rsqrt(ms + EPS) * idxg_ref[...]
    ki_lo = pltpu.roll(ki_hi, 64, 1)
    ki_o[...] = ki_lo[:, :IDX_DIM]
    kip0_o[...] = jnp.where(hi, 0.0, ki_lo).astype(BF16)
    kip1_o[...] = jnp.where(hi, ki_hi, 0.0).astype(BF16)
    kcat_o[:, :KV_LORA] = ckv_n.astype(BF16)
    kcat_o[:, KV_LORA:] = jnp.where(hi, 0.0, rot).astype(BF16)

    kd = kd_ref[...]
    vd = vd_ref[...]
    kd_o[...] = kd
    vd_o[...] = vd
    kdb_o[...] = kd.astype(BF16)
    vdb_o[...] = vd.astype(BF16)

    cqn = _rms(cq_ref[...], qag_ref[...]).astype(BF16)
    qall = _dot(cqn, wuq_ref[...])
    nope_w = MLA_HEADS * HEAD_DIM
    for h in range(MLA_HEADS):
        qn = qall[:, h * HEAD_DIM:(h + 1) * HEAD_DIM].astype(BF16)
        qcat_o[h, :, :KV_LORA] = _dot(qn, wk_ref[h]).astype(qcat_o.dtype)
    for c in range(MLA_HEADS // 2):
        rc = rope128(qall[:, nope_w + 128 * c:nope_w + 128 * (c + 1)])
        qcat_o[2 * c, :, KV_LORA:] = jnp.where(hi, 0.0, rc).astype(qcat_o.dtype)
        qcat_o[2 * c + 1, :, KV_LORA:] = jnp.where(hi, 0.0, pltpu.roll(rc, 64, 1)).astype(qcat_o.dtype)


def _post(z, cos_t, sin_t, qag, kvag, idxg128, wuq, wk, q_dtype):
    n = z.shape[0]
    tm = min(256, n)
    nt = cos_t.shape[0] // tm
    row = lambda c, w: pl.BlockSpec((tm, w), lambda i, c=c, w=w: (i, c // w))
    full2 = lambda a: pl.BlockSpec(a.shape, lambda i: (0, 0))
    full3 = lambda a: pl.BlockSpec(a.shape, lambda i: (0, 0, 0))
    tab = pl.BlockSpec((tm, 128), lambda i: (i % nt, 0))
    out_shapes = [
        jax.ShapeDtypeStruct((n, KV_LORA), F32), jax.ShapeDtypeStruct((n, QK_ROPE), F32),
        jax.ShapeDtypeStruct((n, IDX_DIM), F32), jax.ShapeDtypeStruct((n, 256), F32),
        jax.ShapeDtypeStruct((n, 256), F32), jax.ShapeDtypeStruct((n, 384), BF16),
        jax.ShapeDtypeStruct((n, 128), BF16), jax.ShapeDtypeStruct((n, 128), BF16),
        jax.ShapeDtypeStruct((n, 256), BF16), jax.ShapeDtypeStruct((n, 256), BF16),
        jax.ShapeDtypeStruct((MLA_HEADS, n, 384), q_dtype)]
    o2 = lambda w: pl.BlockSpec((tm, w), lambda i: (i, 0))
    out_specs = [o2(KV_LORA), o2(QK_ROPE), o2(IDX_DIM), o2(256), o2(256), o2(384), o2(128), o2(128), o2(256),
                 o2(256), pl.BlockSpec((MLA_HEADS, tm, 384), lambda i: (0, i, 0))]
    return pl.pallas_call(
        _post_kernel,
        out_shape=out_shapes,
        grid=(n // tm,),
        in_specs=[row(C_Q, 512), row(C_KV, 256), row(C_KRKI, 128), row(C_KD, 256), row(C_VD, 256), tab, tab,
                  full2(qag), full2(kvag), full2(idxg128), full2(wuq), full3(wk)],
        out_specs=out_specs,
        compiler_params=_cparams(("parallel",)),
        name="post",
    )(z, z, z, z, z, cos_t, sin_t, qag, kvag, idxg128, wuq, wk)


def _top16(x):
    return lax.bitcast_convert_type(lax.bitcast_convert_type(x, I32) & np.int32(-65536), F32)


def _count_lanes(acc):
    return jnp.sum(acc.astype(F32), axis=-1, keepdims=True).astype(I32)


def _topk_threshold(count_hi, count_f32, k, rows):
    sign = np.int32(-2 ** 31)
    key_neg_inf = np.int32(-2139095041)

    def to_float(u):
        key = u ^ sign
        bits = jnp.where(key >= 0, key, key ^ np.int32(0x7FFFFFFF))
        return key, lax.bitcast_convert_type(bits, F32)

    def step(b, prefix, done, coarse):
        cand = prefix | jnp.left_shift(np.int32(1), jnp.asarray(31 - b, I32))
        key, cand_f = to_float(cand)
        n = count_hi(_top16(cand_f)) if coarse else count_f32(cand_f)
        ok = (n >= k) | (key <= key_neg_inf)
        return jnp.where(ok, cand, prefix), done | jnp.where(n == k, 1, 0)

    zero = jnp.zeros((rows, 1), I32)
    prefix, done = lax.fori_loop(0, 16, lambda b, c: step(b, c[0], c[1], True), (zero, zero))

    def fine(c):
        b, prefix, done, _ = c
        prefix, done = step(b, prefix, done, False)
        return b + 1, prefix, done, jnp.min(done)

    _, prefix, done, _ = lax.while_loop(lambda c: (c[0] < 32) & (c[3] == 0), fine,
                                        (jnp.int32(16), prefix, done, jnp.min(done)))
    key, thr = to_float(prefix)
    return jnp.where(key < key_neg_inf, NEG_INF, thr), done


def _tie_cutoff(count_eq_before, need, nbits, rows):
    def step(b, p):
        cand = p | jnp.left_shift(np.int32(1), jnp.asarray(nbits - 1 - b, I32))
        return jnp.where(count_eq_before(cand) < need, cand, p)
    return lax.fori_loop(0, nbits, step, jnp.zeros((rows, 1), I32))


def _t5_bias_tile(dist, t5_ref, heads):
    ge = [dist >= T5_BUCKET_START[b] for b in range(1, NUM_BUCKETS)]
    out = []
    for h in heads:
        bias = jnp.full(dist.shape, t5_ref[0, h], F32)
        for b in range(1, NUM_BUCKETS):
            bias = jnp.where(ge[b - 1], t5_ref[b, h], bias)
        out.append(bias)
    return out


def _softmax_update(s, m, l):
    m_new = jnp.maximum(m, jnp.max(s, axis=-1, keepdims=True))
    m_use = jnp.where(m_new == NEG_INF, 0.0, m_new)
    alpha = jnp.exp(m - m_use)
    p = jnp.exp(s - m_use)
    return m_new, alpha * l + jnp.sum(p, axis=-1, keepdims=True), alpha, p


def _prompt_attn_kernel(t5_ref, qcat_ref, qd_ref, qi_ref, wi_ref, kcat_ref, kip0_ref, kip1_ref, kd_ref, vd_ref,
                        wv_ref, omla_ref, od_ref, sc_ref, hi_ref, acc_ref, accd_ref, band_ref, *, t, topk):
    i = pl.program_id(1)
    rows_m = MLA_HEADS * t
    row = lax.broadcasted_iota(I32, (t, t), 0)
    col = lax.broadcasted_iota(I32, (t, t), 1)
    vis = col <= row

    @pl.when((pl.program_id(0) == 0) & (i == 0))
    def _():
        left = _t5_bias_tile(row - col + t, t5_ref, range(DSA_HEADS))
        diag = _t5_bias_tile(row - col, t5_ref, range(DSA_HEADS))
        for h in range(DSA_HEADS):
            band_ref[h, :, :t] = left[h]
            band_ref[h, :, t:] = diag[h]

    qall = qcat_ref[...].reshape(rows_m, 384)
    qi_b = qi_ref[...].astype(BF16)
    wsc = wi_ref[...] * IDX_W_SCALE
    acc_ref[...] = jnp.zeros_like(acc_ref)

    def kv_step(j, carry, masked):
        m, l = carry
        start = pl.multiple_of(j * t, t)
        kc = kcat_ref[pl.ds(start, t), :]
        rg = rows_m // MLA_GROUPS
        hpg = IDX_HEADS // MLA_GROUPS
        kp = (kip0_ref[pl.ds(start, t), :], kip1_ref[pl.ds(start, t), :])
        sc = jnp.zeros((t, t), F32)
        m_new, l_new = [], []
        for g in range(MLA_GROUPS):
            rows = slice(g * rg, (g + 1) * rg)
            s = _dot_nt(qall[rows], kc)
            for h in range(g * hpg, (g + 1) * hpg):
                d = _dot_nt(qi_b[:, 128 * (h // 2):128 * (h // 2 + 1)], kp[h % 2])
                sc = sc + jnp.maximum(d, 0.0) * wsc[:, h:h + 1]
            if masked:
                s = jnp.where(vis[None], s.reshape(rg // t, t, t), NEG_INF).reshape(rg, t)
            m_g = jnp.maximum(m[g], jnp.max(s, axis=-1, keepdims=True))
            alpha = jnp.exp2((m[g] - m_g) * MLA_EXP2)
            p = jnp.exp2((s - m_g) * MLA_EXP2)
            l_new.append(alpha * l[g] + jnp.sum(p, axis=-1, keepdims=True))
            m_new.append(m_g)
            acc_ref[rows, :] = acc_ref[rows, :] * alpha + _dot(p.astype(BF16), kc[:, :KV_LORA])
        m_new, l_new = tuple(m_new), tuple(l_new)
        if masked:
            sc = jnp.where(vis, sc, NEG_INF)
        sc_ref[j] = sc
        hi_ref[j] = _top16(sc).astype(BF16)
        return m_new, l_new

    rows_g = rows_m // MLA_GROUPS
    init = (tuple(jnp.full((rows_g, 1), NEG_INF, F32) for _ in range(MLA_GROUPS)),
            tuple(jnp.zeros((rows_g, 1), F32) for _ in range(MLA_GROUPS)))
    carry = lax.fori_loop(0, i, functools.partial(kv_step, masked=False), init)
    m_mla, l_mla = kv_step(i, carry, masked=True)

    lanes = t // 128
    col_t = lax.broadcasted_iota(I32, (t, t), 1)

    def count_tiles(pred, dtype):
        def body(j, acc):
            hit = jnp.where(pred(j), jnp.ones((), dtype), jnp.zeros((), dtype))
            for u in range(lanes):
                acc = acc + hit[:, u * 128:(u + 1) * 128]
            return acc
        return _count_lanes(lax.fori_loop(0, i + 1, body, jnp.zeros((t, 128), dtype)))

    def count_hi(c):
        cb = c.astype(BF16)
        return count_tiles(lambda j: hi_ref[j] >= cb, BF16)

    thr, done = _topk_threshold(count_hi, lambda c: count_tiles(lambda j: sc_ref[j] >= c, I32), topk, t)
    thr = jnp.maximum(thr, -F32_MAX)

    @pl.when(jnp.min(done) == 0)
    def _():
        need = topk - count_tiles(lambda j: sc_ref[j] > thr, I32)
        cut = _tie_cutoff(lambda p: count_tiles(lambda j: (sc_ref[j] == thr) & (j * t + col_t < p), I32),
                          need, max(1, (sc_ref.shape[0] * t - 1).bit_length()), t)

        def drop(j, _):
            x = sc_ref[j]
            sc_ref[j] = jnp.where((x == thr) & (j * t + col_t > cut), NEG_INF, x)
            return 0
        lax.fori_loop(0, i + 1, drop, 0)

    qd_b = qd_ref[...].astype(BF16)
    qg = [jnp.concatenate([qd_b[:, (2 * g) * 128:(2 * g + 1) * 128],
                           qd_b[:, (2 * g + 1) * 128:(2 * g + 2) * 128]], axis=0) for g in range(DSA_KV_HEADS)]
    accd_ref[...] = jnp.zeros_like(accd_ref)

    def dsa_step(j, carry, kind):
        start = pl.multiple_of(j * t, t)
        sel = sc_ref[j] >= thr
        new = []
        for g in range(DSA_KV_HEADS):
            m, l = carry[g]
            kd = kd_ref[pl.ds(start, t), g * 128:(g + 1) * 128]
            vd = vd_ref[pl.ds(start, t), g * 128:(g + 1) * 128]
            s = _dot_nt(qg[g], kd) * DSA_SCALE
            parts = []
            for r in range(2):
                h = 2 * g + r
                if kind == "far":
                    b = t5_ref[NUM_BUCKETS - 1, h]
                elif kind == "left":
                    b = band_ref[h, :, :t]
                else:
                    b = band_ref[h, :, t:]
                parts.append(jnp.where(sel, s[r * t:(r + 1) * t] + b, NEG_INF))
            s = jnp.concatenate(parts, axis=0)
            m_new, l_new, alpha, p = _softmax_update(s, m, l)
            accd_ref[g] = accd_ref[g] * alpha + _dot(p.astype(BF16), vd)
            new.append((m_new, l_new))
        return tuple(new)

    init_d = tuple((jnp.full((2 * t, 1), NEG_INF, F32), jnp.zeros((2 * t, 1), F32)) for _ in range(DSA_KV_HEADS))
    carry_d = lax.fori_loop(0, jnp.maximum(i - 1, 0), functools.partial(dsa_step, kind="far"), init_d)
    carry_d = lax.fori_loop(jnp.maximum(i - 1, 0), i, functools.partial(dsa_step, kind="left"), carry_d)
    carry_d = dsa_step(i, carry_d, "diag")

    for g in range(DSA_KV_HEADS):
        o = accd_ref[g] / carry_d[g][1]
        for r in range(2):
            h = 2 * g + r
            od_ref[:, h * 128:(h + 1) * 128] = o[r * t:(r + 1) * t]
    heads_g = MLA_HEADS // MLA_GROUPS
    for g in range(MLA_GROUPS):
        lat = (acc_ref[g * rows_g:(g + 1) * rows_g, :] / l_mla[g]).astype(BF16)
        for hh in range(heads_g):
            h = g * heads_g + hh
            omla_ref[:, h * 128:(h + 1) * 128] = _dot(lat[hh * t:(hh + 1) * t], wv_ref[h])


def _prompt_attn(t5, qcat8, z, kcat, kip0, kip1, kdb, vdb, wv, *, batch, seq, t, topk):
    assert t >= T5_FAR and seq % t == 0 and seq // 128 <= 256
    n = batch * seq
    nq = seq // t
    qrow = lambda c, w: pl.BlockSpec((t, w), lambda b, i, c=c, w=w: (b * nq + i, c // w))
    kfull = lambda w: pl.BlockSpec((seq, w), lambda b, i: (b, 0))
    kern = functools.partial(_prompt_attn_kernel, t=t, topk=topk)
    return pl.pallas_call(
        kern,
        out_shape=[jax.ShapeDtypeStruct((n, MLA_HEADS * HEAD_DIM), F32),
                   jax.ShapeDtypeStruct((n, DSA_HEADS * HEAD_DIM), F32)],
        grid=(batch, nq),
        in_specs=[pl.BlockSpec(memory_space=pltpu.SMEM),
                  pl.BlockSpec((MLA_HEADS, t, 384), lambda b, i: (0, b * nq + i, 0)),
                  qrow(C_QD, 512), qrow(C_QI, 1024), qrow(C_WI, 128),
                  kfull(384), kfull(128), kfull(128), kfull(256), kfull(256),
                  pl.BlockSpec(wv.shape, lambda b, i: (0, 0, 0))],
        out_specs=[pl.BlockSpec((t, MLA_HEADS * HEAD_DIM), lambda b, i: (b * nq + i, 0)),
                   pl.BlockSpec((t, DSA_HEADS * HEAD_DIM), lambda b, i: (b * nq + i, 0))],
        scratch_shapes=[pltpu.VMEM((nq, t, t), F32),
                        pltpu.VMEM((nq, t, t), BF16),
                        pltpu.VMEM((MLA_HEADS * t, KV_LORA), F32),
                        pltpu.VMEM((DSA_KV_HEADS, 2 * t, HEAD_DIM), F32),
                        pltpu.VMEM((DSA_HEADS, t, 2 * t), F32)],
        compiler_params=_cparams(("arbitrary", "arbitrary")),
        name="prompt_attn",
    )(t5, qcat8, z, z, z, kcat, kip0, kip1, kdb, vdb, wv)


def _pad_rows(x, rows):
    return jnp.concatenate([x, jnp.zeros((rows - x.shape[0], x.shape[1]), x.dtype)], axis=0)


N_SLOTS = 4


def _chunk_stream(n, nb, nch, chunk_copies):
    total = nb * nch

    def run(g, action):
        gc = jnp.minimum(g, total - 1)
        for cp in chunk_copies(gc // nch, gc % nch, g % N_SLOTS):
            getattr(cp, action)()

    def prologue():
        @pl.when(n == 0)
        def _():
            for g in range(N_SLOTS - 1):
                run(g, "start")

    def advance(c):
        g = n * nch + c
        run(g + N_SLOTS - 1, "start")
        run(g, "wait")
        return g % N_SLOTS

    def epilogue():
        @pl.when(n == nb - 1)
        def _():
            for g in range(total, total + N_SLOTS - 1):
                run(g, "wait")

    return prologue, advance, epilogue


def _sample_mla_kernel(pt_ref, qcat_ref, qi_ref, we_ref, wo_ref, ckvn_ref, krn_ref, kin_ref, wv_ref,
                       ckv_hbm, krt_hbm, kit_hbm, sc_ref, omla_ref, ckv_buf, krt_buf, kit_buf, sem, *, c_pages, nb):
    n = pl.program_id(0)
    nch = pt_ref.shape[1] // c_pages
    cw = c_pages * PAGE_SIZE
    past = nch * cw
    t_new = ckvn_ref.shape[0]
    rows_m = MLA_HEADS * t_new

    def chunk_copies(seq, c, slot):
        cps = []
        for u in range(c_pages):
            page = pt_ref[seq, c * c_pages + u]
            rows = pl.ds(u * PAGE_SIZE, PAGE_SIZE)
            cps.append(pltpu.make_async_copy(ckv_hbm.at[0, page], ckv_buf.at[slot, rows, :], sem.at[0, slot]))
            cps.append(pltpu.make_async_copy(krt_hbm.at[0, page], krt_buf.at[slot, :, rows], sem.at[1, slot]))
            cps.append(pltpu.make_async_copy(kit_hbm.at[0, page], kit_buf.at[slot, :, rows], sem.at[2, slot]))
        return cps

    prologue, advance, epilogue = _chunk_stream(n, nb, nch, chunk_copies)
    prologue()

    q = qcat_ref[...].reshape(rows_m, 384).astype(BF16)
    ql = q[:, :KV_LORA]
    qr = q[:, KV_LORA:KV_LORA + QK_ROPE]
    rows_i = qi_ref.shape[1]
    q_idx = qi_ref[...].reshape(2 * rows_i, IDX_DIM).astype(BF16)
    w_idx = jnp.concatenate([we_ref[...], wo_ref[...]], axis=0) * IDX_W_SCALE

    def idx_scores(dots):
        d = jnp.maximum(dots, 0.0) * w_idx
        return jnp.sum(d.reshape(IDX_HEADS, t_new, d.shape[-1]), axis=0)

    def chunk(c, carry):
        m, l, acc = carry
        slot = advance(c)
        ck = ckv_buf[slot].astype(BF16)
        s = (_dot_nt(ql, ck) + _dot(qr, krt_buf[slot].astype(BF16))) * MLA_SCALE
        m, l, alpha, p = _softmax_update(s, m, l)
        acc = acc * alpha + _dot(p.astype(BF16), ck)
        kit = kit_buf[slot].astype(BF16)
        sc_ref[:, pl.ds(pl.multiple_of(c * cw, cw), cw)] = idx_scores(_dot(q_idx, kit))
        return m, l, acc

    init = (jnp.full((rows_m, 1), NEG_INF, F32), jnp.zeros((rows_m, 1), F32), jnp.zeros((rows_m, KV_LORA), F32))
    m, l, acc = lax.fori_loop(0, nch, chunk, init)
    epilogue()

    tok = lax.broadcasted_iota(I32, (t_new, PAGE_SIZE), 0)
    col = lax.broadcasted_iota(I32, (t_new, PAGE_SIZE), 1)
    vis = col <= tok
    ck = _pad_rows(ckvn_ref[...], PAGE_SIZE).astype(BF16)
    kr = _pad_rows(krn_ref[...], PAGE_SIZE).astype(BF16)
    ki = _pad_rows(kin_ref[...], PAGE_SIZE).astype(BF16)
    s = (_dot_nt(ql, ck) + _dot_nt(qr, kr)) * MLA_SCALE
    s = jnp.where(vis[None], s.reshape(MLA_HEADS, t_new, PAGE_SIZE), NEG_INF).reshape(rows_m, PAGE_SIZE)
    m, l, alpha, p = _softmax_update(s, m, l)
    acc = acc * alpha + _dot(p.astype(BF16), ck)
    sc_ref[:, past:] = jnp.where(vis, idx_scores(_dot_nt(q_idx, ki)), NEG_INF)
    lat = (acc / l).astype(BF16)
    for h in range(MLA_HEADS):
        omla_ref[:, h * 128:(h + 1) * 128] = _dot(lat[h * t_new:(h + 1) * t_new], wv_ref[h])


def _sample_mla(page_table, qcat8, qi_eo, w_even, w_odd, ckv_new, kr_new, ki_new, wv, ckv_pool, krt_pool, kit_pool,
                *, nb, t_new, c_pages):
    npages = page_table.shape[1]
    nch = npages // c_pages
    assert nch * c_pages == npages
    cw = c_pages * PAGE_SIZE
    width = npages * PAGE_SIZE + PAGE_SIZE
    rows_i = (IDX_HEADS // 2) * t_new
    any_spec = pl.BlockSpec(memory_space=pl.ANY)
    in_specs = [pl.BlockSpec((MLA_HEADS, t_new, 384), lambda n, pt: (0, n, 0)),
                pl.BlockSpec((None, 2, rows_i, IDX_DIM), lambda n, pt: (n, 0, 0, 0)),
                pl.BlockSpec((None, rows_i, 1), lambda n, pt: (n, 0, 0)),
                pl.BlockSpec((None, rows_i, 1), lambda n, pt: (n, 0, 0)),
                pl.BlockSpec((t_new, KV_LORA), lambda n, pt: (n, 0)),
                pl.BlockSpec((t_new, QK_ROPE), lambda n, pt: (n, 0)),
                pl.BlockSpec((t_new, IDX_DIM), lambda n, pt: (n, 0)),
                pl.BlockSpec(wv.shape, lambda n, pt: (0, 0, 0)),
                any_spec, any_spec, any_spec]
    grid_spec = pltpu.PrefetchScalarGridSpec(
        num_scalar_prefetch=1, grid=(nb,), in_specs=in_specs,
        out_specs=[pl.BlockSpec((None, t_new, width), lambda n, pt: (n, 0, 0)),
                   pl.BlockSpec((t_new, MLA_HEADS * HEAD_DIM), lambda n, pt: (n, 0))],
        scratch_shapes=[pltpu.VMEM((N_SLOTS, cw, KV_LORA), F32), pltpu.VMEM((N_SLOTS, QK_ROPE, cw), F32),
                        pltpu.VMEM((N_SLOTS, IDX_DIM, cw), F32), pltpu.SemaphoreType.DMA((3, N_SLOTS))])
    return pl.pallas_call(
        functools.partial(_sample_mla_kernel, c_pages=c_pages, nb=nb),
        out_shape=[jax.ShapeDtypeStruct((nb, t_new, width), F32),
                   jax.ShapeDtypeStruct((nb * t_new, MLA_HEADS * HEAD_DIM), F32)],
        grid_spec=grid_spec,
        compiler_params=_cparams(("arbitrary",)),
        name="sample_mla",
    )(page_table, qcat8, qi_eo, w_even, w_odd, ckv_new, kr_new, ki_new, wv, ckv_pool, krt_pool, kit_pool)


def _sample_thr_kernel(sc_ref, thr_ref, cut_ref, hi_ref, *, topk, chunk):
    nseq, t_new, width = sc_ref.shape
    rows = nseq * t_new
    lanes = chunk // 128
    col_c = lax.broadcasted_iota(I32, (rows, chunk), 1)

    def load(c_idx):
        return sc_ref[:, :, pl.ds(pl.multiple_of(c_idx * chunk, chunk), chunk)].reshape(rows, chunk)

    def fill(c_idx, _):
        hi_ref[:, pl.ds(pl.multiple_of(c_idx * chunk, chunk), chunk)] = _top16(load(c_idx)).astype(BF16)
        return 0
    lax.fori_loop(0, width // chunk, fill, 0)

    def count_chunks(pred, dtype):
        def body(c_idx, acc):
            hit = jnp.where(pred(c_idx), jnp.ones((), dtype), jnp.zeros((), dtype))
            for u in range(lanes):
                acc = acc + hit[:, u * 128:(u + 1) * 128]
            return acc
        return _count_lanes(lax.fori_loop(0, width // chunk, body, jnp.zeros((rows, 128), dtype)))

    def count_hi(c):
        cb = c.astype(BF16)
        return count_chunks(lambda ci: hi_ref[:, pl.ds(pl.multiple_of(ci * chunk, chunk), chunk)] >= cb, BF16)

    thr, done = _topk_threshold(count_hi, lambda c: count_chunks(lambda ci: load(ci) >= c, I32), topk, rows)
    thr = jnp.maximum(thr, -F32_MAX)
    thr_ref[...] = thr
    cut_ref[...] = jnp.full((rows, 1), np.iinfo(np.int32).max, I32)

    @pl.when(jnp.min(done) == 0)
    def _():
        need = topk - count_chunks(lambda ci: load(ci) > thr, I32)
        cut_ref[...] = _tie_cutoff(
            lambda p: count_chunks(lambda ci: (load(ci) == thr) & (ci * chunk + col_c < p), I32),
            need, max(1, (width - 1).bit_length()), rows)


def _sample_thr(scores, *, topk, nseq=16):
    nb, t_new, width = scores.shape
    nseq = min(nseq, nb)
    lanes = width // 128
    assert lanes <= 256
    chunk = 128 * max(d for d in range(1, 5) if lanes % d == 0)
    rows = nseq * t_new
    return pl.pallas_call(
        functools.partial(_sample_thr_kernel, topk=topk, chunk=chunk),
        out_shape=[jax.ShapeDtypeStruct((nb * t_new, 1), F32), jax.ShapeDtypeStruct((nb * t_new, 1), I32)],
        grid=(nb // nseq,),
        in_specs=[pl.BlockSpec((nseq, t_new, width), lambda i: (i, 0, 0))],
        out_specs=[pl.BlockSpec((rows, 1), lambda i: (i, 0)), pl.BlockSpec((rows, 1), lambda i: (i, 0))],
        scratch_shapes=[pltpu.VMEM((rows, width), BF16)],
        compiler_params=_cparams(("parallel",)),
        name="sample_thr",
    )(scores)


def _sample_dsa_kernel(pt_ref, t5_ref, qd_ref, sc_ref, thr_ref, cut_ref, kdn_ref, vdn_ref, kd_hbm, vd_hbm, od_ref,
                       kd_buf, vd_buf, sem, *, c_pages, nb):
    n = pl.program_id(0)
    nch = pt_ref.shape[1] // c_pages
    cw = c_pages * PAGE_SIZE
    past = nch * cw
    t_new = qd_ref.shape[0]

    def chunk_copies(seq, c, slot):
        cps = []
        for u in range(c_pages):
            page = pt_ref[seq, c * c_pages + u]
            rows = pl.ds(u * PAGE_SIZE, PAGE_SIZE)
            for g in range(DSA_KV_HEADS):
                cps.append(pltpu.make_async_copy(kd_hbm.at[0, page, :, g, :], kd_buf.at[slot, g, rows, :], sem.at[0, slot]))
                cps.append(pltpu.make_async_copy(vd_hbm.at[0, page, :, g, :], vd_buf.at[slot, g, rows, :], sem.at[1, slot]))
        return cps

    prologue, advance, epilogue = _chunk_stream(n, nb, nch, chunk_copies)
    prologue()

    qd_b = qd_ref[...].astype(BF16)
    qg = [jnp.concatenate([qd_b[:, (2 * g) * 128:(2 * g + 1) * 128],
                           qd_b[:, (2 * g + 1) * 128:(2 * g + 2) * 128]], axis=0) for g in range(DSA_KV_HEADS)]
    thr = thr_ref[...]
    cut = cut_ref[...]

    def selected(start, width):
        x = sc_ref[:, pl.ds(start, width)]
        col = start + lax.broadcasted_iota(I32, (t_new, width), 1)
        return (x > thr) | ((x == thr) & (col <= cut))

    def attend(carry, k_of, v_of, sel, dist):
        if dist is not None:
            bias = _t5_bias_tile(dist, t5_ref, range(DSA_HEADS))
        new = []
        for g in range(DSA_KV_HEADS):
            m, l, acc = carry[g]
            s = _dot_nt(qg[g], k_of(g)) * DSA_SCALE
            parts = []
            for r in range(2):
                h = 2 * g + r
                b = bias[h] if dist is not None else t5_ref[NUM_BUCKETS - 1, h]
                parts.append(jnp.where(sel, s[r * t_new:(r + 1) * t_new] + b, NEG_INF))
            m_new, l_new, alpha, p = _softmax_update(jnp.concatenate(parts, axis=0), m, l)
            new.append((m_new, l_new, acc * alpha + _dot(p.astype(BF16), v_of(g))))
        return tuple(new)

    def chunk(c, carry, near):
        slot = advance(c)
        start = pl.multiple_of(c * cw, cw)
        dist = None
        if near:
            tok = lax.broadcasted_iota(I32, (t_new, cw), 0)
            col = lax.broadcasted_iota(I32, (t_new, cw), 1)
            dist = (past + tok) - (start + col)
        return attend(carry, lambda g: kd_buf[slot, g].astype(BF16), lambda g: vd_buf[slot, g].astype(BF16),
                      selected(start, cw), dist)

    init = tuple((jnp.full((2 * t_new, 1), NEG_INF, F32), jnp.zeros((2 * t_new, 1), F32),
                  jnp.zeros((2 * t_new, HEAD_DIM), F32)) for _ in range(DSA_KV_HEADS))
    carry = lax.fori_loop(0, nch - 1, functools.partial(chunk, near=False), init)
    carry = chunk(nch - 1, carry, near=True)
    epilogue()

    tok = lax.broadcasted_iota(I32, (t_new, PAGE_SIZE), 0)
    col = lax.broadcasted_iota(I32, (t_new, PAGE_SIZE), 1)
    kdn = _pad_rows(kdn_ref[...], PAGE_SIZE).astype(BF16)
    vdn = _pad_rows(vdn_ref[...], PAGE_SIZE).astype(BF16)
    carry = attend(carry, lambda g: kdn[:, g * 128:(g + 1) * 128], lambda g: vdn[:, g * 128:(g + 1) * 128],
                   selected(past, PAGE_SIZE), tok - col)
    for g in range(DSA_KV_HEADS):
        o = carry[g][2] / carry[g][1]
        for r in range(2):
            h = 2 * g + r
            od_ref[:, h * 128:(h + 1) * 128] = o[r * t_new:(r + 1) * t_new]


def _sample_dsa(page_table, t5, z, scores, thr, cut, kd_new, vd_new, kd_pool, vd_pool, *, nb, t_new, c_pages):
    npages = page_table.shape[1]
    nch = npages // c_pages
    assert nch * c_pages == npages and c_pages * PAGE_SIZE >= T5_FAR
    cw = c_pages * PAGE_SIZE
    width = scores.shape[-1]
    any_spec = pl.BlockSpec(memory_space=pl.ANY)
    in_specs = [pl.BlockSpec(memory_space=pltpu.SMEM),
                pl.BlockSpec((t_new, 512), lambda n, pt: (n, C_QD // 512)),
                pl.BlockSpec((None, t_new, width), lambda n, pt: (n, 0, 0)),
                pl.BlockSpec((t_new, 1), lambda n, pt: (n, 0)),
                pl.BlockSpec((t_new, 1), lambda n, pt: (n, 0)),
                pl.BlockSpec((t_new, 256), lambda n, pt: (n, 0)),
                pl.BlockSpec((t_new, 256), lambda n, pt: (n, 0)),
                any_spec, any_spec]
    grid_spec = pltpu.PrefetchScalarGridSpec(
        num_scalar_prefetch=1, grid=(nb,), in_specs=in_specs,
        out_specs=pl.BlockSpec((t_new, DSA_HEADS * HEAD_DIM), lambda n, pt: (n, 0)),
        scratch_shapes=[pltpu.VMEM((N_SLOTS, DSA_KV_HEADS, cw, HEAD_DIM), F32),
                        pltpu.VMEM((N_SLOTS, DSA_KV_HEADS, cw, HEAD_DIM), F32),
                        pltpu.SemaphoreType.DMA((2, N_SLOTS))])
    return pl.pallas_call(
        functools.partial(_sample_dsa_kernel, c_pages=c_pages, nb=nb),
        out_shape=jax.ShapeDtypeStruct((nb * t_new, DSA_HEADS * HEAD_DIM), F32),
        grid_spec=grid_spec,
        compiler_params=_cparams(("arbitrary",)),
        name="sample_dsa",
    )(page_table, t5, z, scores, thr, cut, kd_new, vd_new, kd_pool, vd_pool)


def _mem_attn_kernel(q_ref, k_ref, v_ref, o_ref):
    for s in range(q_ref.shape[0]):
        q = q_ref[s].astype(BF16)
        for h in range(MEM_HEADS):
            k = k_ref[s, :, h, :].astype(BF16)
            v = v_ref[s, :, h, :].astype(BF16)
            logits = _dot_nt(q[:, h * 128:(h + 1) * 128], k) * MEM_SCALE
            p = jnp.exp(logits - jnp.max(logits, axis=-1, keepdims=True))
            o = _dot(p.astype(BF16), v) / jnp.sum(p, axis=-1, keepdims=True)
            o_ref[s, :, h * 128:(h + 1) * 128] = o


def _mem_attn(z3, mem_k, mem_v, *, seqs_per_step, rows_per_step):
    groups, rows, _ = z3.shape
    n_mem = mem_k.shape[1]
    gs, rs = seqs_per_step, rows_per_step
    kv_spec = pl.BlockSpec((gs, n_mem, MEM_HEADS, HEAD_DIM), lambda a, b: (a, 0, 0, 0))
    return pl.pallas_call(
        _mem_attn_kernel,
        out_shape=jax.ShapeDtypeStruct((groups, rows, MEM_HEADS * HEAD_DIM), F32),
        grid=(groups // gs, rows // rs),
        in_specs=[pl.BlockSpec((gs, rs, 512), lambda a, b: (a, b, C_QM // 512)), kv_spec, kv_spec],
        out_specs=pl.BlockSpec((gs, rs, 512), lambda a, b: (a, b, 0)),
        compiler_params=_cparams(("parallel", "arbitrary")),
        name="mem_attn",
    )(z3, mem_k, mem_v)


def _mem_attn_rows_kernel(q_ref, k_ref, v_ref, o_ref):
    gs, t_new, _ = q_ref.shape
    nrow = k_ref.shape[1]
    rows = MEM_HEADS * t_new
    same_head = (lax.broadcasted_iota(I32, (rows, nrow), 0) // t_new
                 == lax.broadcasted_iota(I32, (rows, nrow), 1) % MEM_HEADS)
    for s in range(gs):
        q = q_ref[s]
        qs = jnp.concatenate([q[:, h * 128:(h + 1) * 128] for h in range(MEM_HEADS)], axis=0).astype(BF16)
        logits = jnp.where(same_head, _dot_nt(qs, k_ref[s].astype(BF16)) * MEM_SCALE, NEG_INF)
        p = jnp.exp(logits - jnp.max(logits, axis=-1, keepdims=True))
        o = _dot(p.astype(BF16), v_ref[s].astype(BF16)) / jnp.sum(p, axis=-1, keepdims=True)
        for h in range(MEM_HEADS):
            o_ref[s, :, h * 128:(h + 1) * 128] = o[h * t_new:(h + 1) * t_new]


def _mem_attn_rows(z3, mem_k, mem_v, *, seqs_per_step):
    groups, t_new, _ = z3.shape
    gs = seqs_per_step
    kv_spec = pl.BlockSpec((gs,) + mem_k.shape[1:], lambda a: (a, 0, 0))
    return pl.pallas_call(
        _mem_attn_rows_kernel,
        out_shape=jax.ShapeDtypeStruct((groups, t_new, MEM_HEADS * HEAD_DIM), F32),
        grid=(groups // gs,),
        in_specs=[pl.BlockSpec((gs, t_new, 512), lambda a: (a, 0, C_QM // 512)), kv_spec, kv_spec],
        out_specs=pl.BlockSpec((gs, t_new, 512), lambda a: (a, 0, 0)),
        compiler_params=_cparams(("parallel",)),
        name="mem_attn_rows",
    )(z3, mem_k, mem_v)


def _silu(g):
    return g * (1.0 / (1.0 + jnp.exp(-g)))


def _merge_kernel(x_ref, omla_ref, od_ref, om_ref, gmla_ref, gdsa_ref, gmem_ref, wout_ref, fg_ref, y_ref):
    o = jnp.concatenate([(omla_ref[...] * _silu(gmla_ref[...])).astype(BF16),
                         (od_ref[...] * _silu(gdsa_ref[...])).astype(BF16),
                         (om_ref[...] * _silu(gmem_ref[...])).astype(BF16)], axis=-1)
    xo = x_ref[...] + _dot(o, wout_ref[...])
    y_ref[...] = _rms(xo, fg_ref[...])


def _merge(x, omla, od, om, z, wout, fg):
    n, d = x.shape
    tm = min(256, n)
    row = lambda w, c=0: pl.BlockSpec((tm, w), lambda i, c=c, w=w: (i, c // w))
    return pl.pallas_call(
        _merge_kernel,
        out_shape=jax.ShapeDtypeStruct((n, d), F32),
        grid=(n // tm,),
        in_specs=[row(d), row(1024), row(512), row(512), row(1024, C_GMLA), row(512, C_GDSA), row(512, C_GMEM),
                  pl.BlockSpec(wout.shape, lambda i: (0, 0)), pl.BlockSpec((1, d), lambda i: (0, 0))],
        out_specs=row(d),
        compiler_params=_cparams(("parallel",)),
        name="merge",
    )(x, omla, od, om, z, z, z, wout, fg)


def _rope_tables(pos):
    half = QK_ROPE // 2
    inv_freq = jnp.exp(-math.log(ROPE_THETA) * jnp.arange(half, dtype=F32) / half)
    ang = pos.astype(F32)[:, None] * inv_freq[None, :]
    cos, sin = jnp.cos(ang), jnp.sin(ang)
    return jnp.tile(cos, (1, 4)), jnp.concatenate([-sin, sin, -sin, sin], axis=1)


def _permute_w_in(w):
    d = w.shape[0]
    seg = lambda a, b: w[:, a:b]
    return jnp.concatenate([
        seg(0, 512), seg(512, 768), seg(768, 832), seg(3904, 3968), seg(3968, 3984), jnp.zeros((d, 112), w.dtype),
        seg(832, 1856), seg(1856, 2368), seg(2368, 2624), seg(2624, 2880), seg(2880, 3904), seg(3984, 4496),
        seg(4496, 5008), seg(5008, 5520)], axis=1).astype(BF16)


def kernel(x_prompt, x_sample, mem_prompt, cache_mla_ckv, cache_mla_krope, cache_dsa_k, cache_dsa_v, cache_idx_k,
           cache_mem_k, cache_mem_v, page_table, t5_bias, norm_g, w_in, qa_norm_g, w_uq, kva_norm_g, w_ukv,
           idx_norm_g, mem_norm_g, w_mem_kv, w_out, final_norm_g, *, t_blk=256, c_pages=32):
    bp, seq, d = x_prompt.shape
    nb, t_new, _ = x_sample.shape
    n_mem = mem_prompt.shape[1]
    npages = page_table.shape[1]
    past = npages * PAGE_SIZE
    depth = norm_g.shape[0]
    assert depth == 1
    l = 0
    t_blk = min(t_blk, seq)
    c_pages = min(c_pages, npages // 2)
    topk_p = min(TOPK_MAX, seq // 4)
    topk_s = min(TOPK_MAX, (past + t_new) // 4)

    w_in_p = _permute_w_in(w_in[l])
    wuq = jnp.concatenate([w_uq[l][:, :, :HEAD_DIM].reshape(Q_LORA, -1),
                           w_uq[l][:, :, HEAD_DIM:].reshape(Q_LORA, -1)], axis=1).astype(BF16)
    wk = jnp.transpose(w_ukv[l][:, :, :HEAD_DIM], (1, 2, 0)).astype(BF16)
    wv = jnp.transpose(w_ukv[l][:, :, HEAD_DIM:], (1, 0, 2)).astype(BF16)
    wout = w_out[l].astype(BF16)
    wmem = w_mem_kv[l].astype(BF16)
    g_row = lambda g: g.reshape(1, -1)
    idxg128 = jnp.concatenate([jnp.zeros((1, 64), F32), g_row(idx_norm_g[l])], axis=1)
    fg = g_row(final_norm_g)

    def branch(x2, cos_t, sin_t, q_dtype):
        z = _in_proj(x2, g_row(norm_g[l]), w_in_p)
        outs = _post(z, cos_t, sin_t, g_row(qa_norm_g[l]), g_row(kva_norm_g[l]), idxg128, wuq, wk, q_dtype)
        return z, outs

    xp2 = x_prompt.reshape(bp * seq, d)
    cos_p, sin_p = _rope_tables(jnp.arange(seq, dtype=I32))
    zp, (ckv_p, krope_p, ki_p, kd_p, vd_p, kcat_p, kip0_p, kip1_p, kdb_p, vdb_p, qcat_p) = branch(xp2, cos_p, sin_p, BF16)
    omla_p, od_p = _prompt_attn(t5_bias, qcat_p, zp, kcat_p, kip0_p, kip1_p, kdb_p, vdb_p, wv,
                                batch=bp, seq=seq, t=t_blk, topk=topk_p)
    zmem = _in_proj(mem_prompt.reshape(bp * n_mem, d), g_row(mem_norm_g[l]), wmem)
    mk_p = zmem[:, :MEM_HEADS * HEAD_DIM].reshape(bp, n_mem, MEM_HEADS, HEAD_DIM)
    mv_p = zmem[:, MEM_HEADS * HEAD_DIM:].reshape(bp, n_mem, MEM_HEADS, HEAD_DIM)
    om_p = _mem_attn(zp.reshape(bp, seq, D_Z), mk_p, mv_p, seqs_per_step=1, rows_per_step=min(512, seq))
    y_prompt = _merge(xp2, omla_p, od_p, om_p.reshape(bp * seq, -1), zp, wout, fg).reshape(bp, seq, d)

    xs2 = x_sample.reshape(nb * t_new, d)
    tm_s = min(256, nb * t_new)
    pos_s = past + (jnp.arange(tm_s, dtype=I32) % t_new)
    cos_s, sin_s = _rope_tables(pos_s)
    zs, (ckv_s, krope_s, ki_s, kd_s, vd_s, _, _, _, _, _, qcat_s) = branch(xs2, cos_s, sin_s, F32)
    wi_s = zs[:, C_WI:C_WI + IDX_HEADS].reshape(nb, t_new, IDX_HEADS // 2, 2)
    rows_i = (IDX_HEADS // 2) * t_new
    w_even = jnp.transpose(wi_s[..., 0], (0, 2, 1)).reshape(nb, rows_i, 1)
    w_odd = jnp.transpose(wi_s[..., 1], (0, 2, 1)).reshape(nb, rows_i, 1)
    qi_eo = jnp.transpose(zs[:, C_QI:C_QI + IDX_HEADS * IDX_DIM].reshape(nb, t_new, IDX_HEADS // 2, 2, IDX_DIM),
                          (0, 3, 2, 1, 4)).reshape(nb, 2, rows_i, IDX_DIM)
    krt_pool = jnp.swapaxes(cache_mla_krope, 2, 3)
    kit_pool = jnp.swapaxes(cache_idx_k, 2, 3)
    scores, omla_s = _sample_mla(page_table, qcat_s, qi_eo, w_even, w_odd, ckv_s, krope_s, ki_s, wv,
                                 cache_mla_ckv, krt_pool, kit_pool, nb=nb, t_new=t_new, c_pages=c_pages)
    thr, cut = _sample_thr(scores, topk=topk_s)
    od_s = _sample_dsa(page_table, t5_bias, zs, scores, thr, cut, kd_s, vd_s, cache_dsa_k, cache_dsa_v,
                       nb=nb, t_new=t_new, c_pages=c_pages)
    mem_rows = lambda a: a.reshape(nb, a.shape[1] * MEM_HEADS, HEAD_DIM)
    om_s = _mem_attn_rows(zs.reshape(nb, t_new, D_Z), mem_rows(cache_mem_k[l]), mem_rows(cache_mem_v[l]),
                          seqs_per_step=min(8, nb))
    y_sample = _merge(xs2, omla_s, od_s, om_s.reshape(nb * t_new, -1), zs, wout, fg).reshape(nb, t_new, d)

    st = lambda a, *shape: a.reshape((1,) + shape)
    return (y_prompt, y_sample,
            st(ckv_p, bp, seq, KV_LORA), st(krope_p, bp, seq, QK_ROPE),
            st(kd_p, bp, seq, DSA_KV_HEADS, HEAD_DIM), st(vd_p, bp, seq, DSA_KV_HEADS, HEAD_DIM),
            st(ki_p, bp, seq, IDX_DIM), st(mk_p, bp, n_mem, MEM_HEADS, HEAD_DIM), st(mv_p, bp, n_mem, MEM_HEADS, HEAD_DIM),
            st(ckv_s, nb, t_new, KV_LORA), st(krope_s, nb, t_new, QK_ROPE),
            st(kd_s, nb, t_new, DSA_KV_HEADS, HEAD_DIM), st(vd_s, nb, t_new, DSA_KV_HEADS, HEAD_DIM),
            st(ki_s, nb, t_new, IDX_DIM))
```
